```python
import math
import jax, jax.numpy as jnp
from jax import lax
import numpy as np


D_MODEL = 2048
BATCH = 4
SEQ = 2048
DEPTH = 2
DEC_BATCH = 8
DEC_SEQ = 4096
PAST_LEN = 128

GRID_W = 64
N_EVEN = (DEPTH + 1) // 2
N_ODD = DEPTH // 2
A_HEADS = 8
A_QK_DIM = 64
A_V_DIM = 2 * A_QK_DIM
SUBLN_EPS = 1e-5
B_HEADS = 8
B_HEAD_DIM = 128
NA_WIN_ROWS = 8
NA_WIN_COLS = 16
C_GROUPS = 4
C_GROUP_DIM = 256
C_WIDTH = C_GROUPS * C_GROUP_DIM
D_WIDTH = 1024
CONV_W = 3
AB_IN = 4 * A_HEADS * A_QK_DIM + A_HEADS * A_V_DIM + 3 * B_HEADS * B_HEAD_DIM
AB_OUT = A_HEADS * A_V_DIM + B_HEADS * B_HEAD_DIM
CD_IN = C_WIDTH + 3 * D_WIDTH
CD_OUT = C_WIDTH + D_WIDTH
T5_BUCKETS = 32
T5_MAX_DIST = 128
Q_BLOCK = 128
N_EXPERTS = 32
N_GROUPS = 8
EXPERTS_PER_GROUP = N_EXPERTS // N_GROUPS
TOP_K = 2
D_EXPERT = 512
EXPERT_BLOCK = 256
EPS = 1e-6

kernel_name = 'hybrid_diffattn_natten_fnet_shortconv_groupmoe_encoder'


def rms_norm(x, g, eps=EPS):
    xf = x.astype(jnp.float32)
    y = xf * lax.rsqrt(jnp.mean(xf * xf, axis=-1, keepdims=True) + eps)
    return (y * g.astype(jnp.float32)).astype(x.dtype)


def t5_bucket(rel):
    half = T5_BUCKETS // 2
    max_exact = half // 2
    sign = (rel > 0).astype(jnp.int32) * half
    n = jnp.abs(rel)
    nf = jnp.maximum(n, 1).astype(jnp.float32)
    large = max_exact + (jnp.log(nf / max_exact) / math.log(T5_MAX_DIST / max_exact)
                         * (half - max_exact)).astype(jnp.int32)
    large = jnp.minimum(large, half - 1)
    return sign + jnp.where(n < max_exact, n, large)


def diff_attention(q1, q2, k1, k2, v, lam, sub_g, lam_init, rel_bias):
    B_, S_, H, d = q1.shape
    nblk = S_ // Q_BLOCK
    scale = d ** -0.5
    kpos = jnp.arange(S_, dtype=jnp.int32)

    def blk_fn(args):
        qb1, qb2, q0 = args
        qpos = q0 + jnp.arange(Q_BLOCK, dtype=jnp.int32)
        bias = rel_bias[t5_bucket(kpos[None, :] - qpos[:, None])].astype(jnp.float32)
        bias = jnp.transpose(bias, (2, 0, 1))[None]
        s1 = jnp.einsum('bqhd,bkhd->bhqk', qb1, k1, preferred_element_type=jnp.float32) * scale + bias
        s2 = jnp.einsum('bqhd,bkhd->bhqk', qb2, k2, preferred_element_type=jnp.float32) * scale + bias
        a = jax.nn.softmax(s1, axis=-1) - lam * jax.nn.softmax(s2, axis=-1)
        return jnp.einsum('bhqk,bkhd->bqhd', a.astype(v.dtype), v)

    q1b = q1.reshape(B_, nblk, Q_BLOCK, H, d).transpose(1, 0, 2, 3, 4)
    q2b = q2.reshape(B_, nblk, Q_BLOCK, H, d).transpose(1, 0, 2, 3, 4)
    starts = jnp.arange(nblk, dtype=jnp.int32) * Q_BLOCK
    o = lax.map(blk_fn, (q1b, q2b, starts))
    o = o.transpose(1, 0, 2, 3, 4).reshape(B_, S_, H, v.shape[-1])
    o = rms_norm(o, sub_g, SUBLN_EPS) * (1.0 - lam_init)
    return o.reshape(B_, S_, H * v.shape[-1])


def neighbourhood_attention(q, k, v, bias_table):
    B_, S_, H, d = q.shape
    rows = S_ // GRID_W
    win_r = min(NA_WIN_ROWS, rows)
    scale = d ** -0.5
    qg = q.reshape(B_, rows, GRID_W, H, d)
    kg = k.reshape(B_, rows, GRID_W, H, d)
    vg = v.reshape(B_, rows, GRID_W, H, d)
    col = jnp.arange(GRID_W, dtype=jnp.int32)
    col_start = jnp.clip(col - NA_WIN_COLS // 2, 0, GRID_W - NA_WIN_COLS)
    col_idx = col_start[:, None] + jnp.arange(NA_WIN_COLS, dtype=jnp.int32)
    dc = col_idx - col[:, None] + (NA_WIN_COLS - 1)

    def row_fn(r):
        rs = jnp.clip(r - win_r // 2, 0, rows - win_r)
        q_row = lax.dynamic_index_in_dim(qg, r, axis=1, keepdims=False)
        k_rows = lax.dynamic_slice_in_dim(kg, rs, win_r, axis=1)
        v_rows = lax.dynamic_slice_in_dim(vg, rs, win_r, axis=1)
        k_win = k_rows[:, :, col_idx]
        v_win = v_rows[:, :, col_idx]
        s = jnp.einsum('bqhd,brqchd->bhqrc', q_row, k_win, preferred_element_type=jnp.float32) * scale
        dr = rs + jnp.arange(win_r, dtype=jnp.int32) - r + (NA_WIN_ROWS - 1)
        bias = bias_table[:, dr[None, :, None], dc[:, None, :]].astype(jnp.float32)
        s = s + bias[None]
        p = jax.nn.softmax(s.reshape(B_, H, GRID_W, win_r * NA_WIN_COLS), axis=-1)
        p = p.reshape(B_, H, GRID_W, win_r, NA_WIN_COLS).astype(v.dtype)
        return jnp.einsum('bhqrc,brqchd->bqhd', p, v_win)

    o = lax.map(row_fn, jnp.arange(rows, dtype=jnp.int32))
    return o.transpose(1, 0, 2, 3, 4).reshape(B_, S_, H * d)


def even_mixer(h, w_in, w_out, lq1, lk1, lq2, lk2, sub_g, na_tab, lam_init, rel_bias):
    B_, S_, _ = h.shape
    proj = h @ w_in
    na = A_HEADS * A_QK_DIM
    nv = A_HEADS * A_V_DIM
    nb = B_HEADS * B_HEAD_DIM
    cuts = np.cumsum([na, na, na, na, nv, nb, nb]).tolist()
    q1, q2, k1, k2, va, qb, kb, vb = jnp.split(proj, cuts, axis=-1)
    lam = (jnp.exp(jnp.sum(lq1.astype(jnp.float32) * lk1.astype(jnp.float32)))
           - jnp.exp(jnp.sum(lq2.astype(jnp.float32) * lk2.astype(jnp.float32))) + lam_init)
    ha = lambda t, dd: t.reshape(B_, S_, -1, dd)
    oa = diff_attention(ha(q1, A_QK_DIM), ha(q2, A_QK_DIM), ha(k1, A_QK_DIM), ha(k2, A_QK_DIM),
                        ha(va, A_V_DIM), lam, sub_g, lam_init, rel_bias)
    ob = neighbourhood_attention(ha(qb, B_HEAD_DIM), ha(kb, B_HEAD_DIM), ha(vb, B_HEAD_DIM), na_tab)
    return jnp.concatenate([oa, ob], axis=-1) @ w_out


def odd_mixer(h, w_in, cw, cb, w_out):
    B_, S_, _ = h.shape
    proj = h @ w_in
    uc, bg, cg, hv = jnp.split(proj, [C_WIDTH, C_WIDTH + D_WIDTH, C_WIDTH + 2 * D_WIDTH], axis=-1)
    uf = uc.astype(jnp.float32).reshape(B_, S_, C_GROUPS, C_GROUP_DIM)
    oc = jnp.fft.fft2(uf, axes=(1, 3), norm='ortho').real.reshape(B_, S_, C_WIDTH).astype(h.dtype)
    u = cg * hv
    up = jnp.pad(u, ((0, 0), (1, 1), (0, 0)))
    y = up[:, :-2] * cw[0] + up[:, 1:-1] * cw[1] + up[:, 2:] * cw[2] + cb
    od = bg * y
    return jnp.concatenate([oc, od], axis=-1) @ w_out


def moe(h, w_router, b_router, w_gu, w_dn):
    B_, S_, D_ = h.shape
    T = B_ * S_
    xt = h.reshape(T, D_)
    s = jax.nn.sigmoid(xt.astype(jnp.float32) @ w_router.astype(jnp.float32))
    sel = (s + b_router.astype(jnp.float32)).reshape(T, N_GROUPS, EXPERTS_PER_GROUP)
    gscore = lax.top_k(sel, TOP_K)[0].sum(-1)
    best = jnp.argmax(gscore, axis=-1).astype(jnp.int32)
    sel_in = jnp.take_along_axis(sel, best[:, None, None], axis=1)[:, 0]
    _, loc = lax.top_k(sel_in, TOP_K)
    eidx = best[:, None] * EXPERTS_PER_GROUP + loc.astype(jnp.int32)
    gw = jnp.take_along_axis(s, eidx, axis=1)
    gw = gw / jnp.sum(gw, axis=-1, keepdims=True)
    TK = T * TOP_K
    nblk = -(-TK // EXPERT_BLOCK) + N_EXPERTS
    nrows = nblk * EXPERT_BLOCK
    flat_e = eidx.reshape(TK)
    flat_t = jnp.arange(TK, dtype=jnp.int32) // TOP_K
    flat_w = gw.reshape(TK)
    order = jnp.argsort(flat_e)
    se = flat_e[order]
    counts = jnp.zeros((N_EXPERTS,), jnp.int32).at[flat_e].add(1)
    padded = (counts + EXPERT_BLOCK - 1) // EXPERT_BLOCK * EXPERT_BLOCK
    pend = jnp.cumsum(padded)
    pstart = pend - padded
    start = jnp.cumsum(counts) - counts
    dest = pstart[se] + jnp.arange(TK, dtype=jnp.int32) - start[se]
    row_tok = jnp.zeros((nrows,), jnp.int32).at[dest].set(flat_t[order])
    row_w = jnp.zeros((nrows,), jnp.float32).at[dest].set(flat_w[order])
    blk_start = jnp.arange(nblk, dtype=jnp.int32) * EXPERT_BLOCK
    blk_e = jnp.minimum(jnp.sum((pend[None, :] <= blk_start[:, None]).astype(jnp.int32), axis=-1),
                        N_EXPERTS - 1)
    xs = xt[row_tok].reshape(nblk, EXPERT_BLOCK, D_)

    def expert_block(args):
        xb, e = args
        g, u = jnp.split(xb @ w_gu[e], 2, axis=-1)
        return (jax.nn.silu(g) * u) @ w_dn[e]

    ys = lax.map(expert_block, (xs, blk_e)).reshape(nrows, D_)
    out = jnp.zeros_like(xt).at[row_tok].add(ys * row_w[:, None].astype(ys.dtype))
    return out.reshape(B_, S_, D_)


def trunk(x, c, w_ada, b_ada, norm_mix_g, norm_ffn_g, w_in_ab, w_out_ab, lambda_q1, lambda_k1,
          lambda_q2, lambda_k2, subln_g, na_bias, w_in_cd, conv_w, conv_b, w_out_cd, rel_bias,
          w_router, b_router, w_gate_up, w_down, final_norm_g):
    cs = jax.nn.silu(c.astype(jnp.float32))
    for l in range(DEPTH):
        mod = (cs @ w_ada[l].astype(jnp.float32) + b_ada[l].astype(jnp.float32)).astype(x.dtype)
        sh1, sc1, g1, sh2, sc2, g2 = jnp.split(mod[:, None, :], 6, axis=-1)
        h = rms_norm(x, norm_mix_g[l]) * (1 + sc1) + sh1
        if l % 2 == 0:
            i = l // 2
            lam_init = 0.8 - 0.6 * math.exp(-0.3 * l)
            m = even_mixer(h, w_in_ab[i], w_out_ab[i], lambda_q1[i], lambda_k1[i], lambda_q2[i],
                           lambda_k2[i], subln_g[i], na_bias[i], lam_init, rel_bias)
        else:
            i = l // 2
            m = odd_mixer(h, w_in_cd[i], conv_w[i], conv_b[i], w_out_cd[i])
        x = x + g1 * m
        h = rms_norm(x, norm_ffn_g[l]) * (1 + sc2) + sh2
        x = x + g2 * moe(h, w_router, b_router, w_gate_up[l], w_down[l])
    return rms_norm(x, final_norm_g)


def setup_inputs(seed: int = 0) -> dict:
    key = jax.random.key(seed)
    ks = jax.random.split(key, 26)
    f32 = jnp.float32
    nrm = lambda k, shape, sc: jax.random.normal(k, shape, f32) * sc
    return {
        'x_prompt': nrm(ks[0], (BATCH, SEQ, D_MODEL), 1.0),
        'x_sample': nrm(ks[1], (DEC_BATCH, DEC_SEQ, D_MODEL), 1.0),
        'c_prompt': nrm(ks[2], (BATCH, D_MODEL), 1.0),
        'c_sample': nrm(ks[3], (DEC_BATCH, D_MODEL), 1.0),
        'w_ada': nrm(ks[4], (DEPTH, D_MODEL, 6 * D_MODEL), 0.5 * D_MODEL ** -0.5),
        'b_ada': nrm(ks[5], (DEPTH, 6 * D_MODEL), 0.01),
        'norm_mix_g': 1.0 + nrm(ks[6], (DEPTH, D_MODEL), 0.02),
        'norm_ffn_g': 1.0 + nrm(ks[7], (DEPTH, D_MODEL), 0.02),
        'w_in_ab': nrm(ks[8], (N_EVEN, D_MODEL, AB_IN), D_MODEL ** -0.5),
        'w_out_ab': nrm(ks[9], (N_EVEN, AB_OUT, D_MODEL), AB_OUT ** -0.5),
        'lambda_q1': nrm(ks[10], (N_EVEN, A_QK_DIM), 0.1),
        'lambda_k1': nrm(ks[11], (N_EVEN, A_QK_DIM), 0.1),
        'lambda_q2': nrm(ks[12], (N_EVEN, A_QK_DIM), 0.1),
        'lambda_k2': nrm(ks[13], (N_EVEN, A_QK_DIM), 0.1),
        'subln_g': 1.0 + nrm(ks[14], (N_EVEN, A_V_DIM), 0.02),
        'na_bias': nrm(ks[15], (N_EVEN, B_HEADS, 2 * NA_WIN_ROWS - 1, 2 * NA_WIN_COLS - 1), 0.1),
        'w_in_cd': nrm(ks[16], (N_ODD, D_MODEL, CD_IN), D_MODEL ** -0.5),
        'conv_w': nrm(ks[17], (N_ODD, CONV_W, D_WIDTH), 0.5),
        'conv_b': nrm(ks[18], (N_ODD, D_WIDTH), 0.01),
        'w_out_cd': nrm(ks[19], (N_ODD, CD_OUT, D_MODEL), CD_OUT ** -0.5),
        'rel_bias': nrm(ks[20], (T5_BUCKETS, A_HEADS), 0.1),
        'w_router': nrm(ks[21], (D_MODEL, N_EXPERTS), D_MODEL ** -0.5),
        'b_router': nrm(ks[22], (N_EXPERTS,), 0.01),
        'w_gate_up': nrm(ks[23], (DEPTH, N_EXPERTS, D_MODEL, 2 * D_EXPERT), D_MODEL ** -0.5),
        'w_down': nrm(ks[24], (DEPTH, N_EXPERTS, D_EXPERT, D_MODEL), D_EXPERT ** -0.5),
        'final_norm_g': 1.0 + nrm(ks[25], (D_MODEL,), 0.02),
    }


def reference(x_prompt, x_sample, c_prompt, c_sample, w_ada, b_ada, norm_mix_g, norm_ffn_g,
              w_in_ab, w_out_ab, lambda_q1, lambda_k1, lambda_q2, lambda_k2, subln_g, na_bias,
              w_in_cd, conv_w, conv_b, w_out_cd, rel_bias, w_router, b_router, w_gate_up,
              w_down, final_norm_g):
    params = (w_ada, b_ada, norm_mix_g, norm_ffn_g, w_in_ab, w_out_ab, lambda_q1, lambda_k1,
              lambda_q2, lambda_k2, subln_g, na_bias, w_in_cd, conv_w, conv_b, w_out_cd, rel_bias,
              w_router, b_router, w_gate_up, w_down, final_norm_g)
    y_prompt = trunk(x_prompt, c_prompt, *params)
    y_sample = trunk(x_sample, c_sample, *params)
    return (y_prompt, y_sample)
```

```python
import functools
import math

import numpy as np
import jax
import jax.numpy as jnp
from jax import lax
from jax.experimental import pallas as pl
from jax.experimental.pallas import tpu as pltpu

F32 = jnp.float32
BF16 = jnp.bfloat16
I32 = jnp.int32

GRID_W = 64
A_HEADS = 8
A_QK = 64
A_V = 128
SUBLN_EPS = 1e-5
B_HEADS = 8
B_HD = 128
NA_WR = 8
NA_WC = 16
C_GROUPS = 4
C_GD = 256
C_WIDTH = 1024
D_WIDTH = 1024
T5_BUCKETS = 32
T5_MAX_DIST = 128
N_EXPERTS = 32
N_GROUPS = 8
EPG = 4
D_EXPERT = 512
EPS = 1e-6
N_PAIRS = 6
N_CLASSES = N_GROUPS * N_PAIRS
PAIR_LO = np.array([0, 0, 0, 1, 1, 2], np.int32)
PAIR_HI = np.array([1, 2, 3, 2, 3, 3], np.int32)

LANES = 128
T5_BAND = 128
VMEM_LIMIT = 56 * 1024 * 1024

NB_PAD = 16
ROW_TILE = 512
ROUTE_TILE = 256
MOE_BLOCK = 256
ATT_TQ = 256


def _dot(a, b):
    return jnp.dot(a, b, preferred_element_type=F32)


def _dot_nt(a, b):
    return lax.dot_general(a, b, (((1,), (1,)), ((), ())), preferred_element_type=F32)


def _split_bf16(x):
    hi = x.astype(BF16)
    lo = (x - hi.astype(F32)).astype(BF16)
    return hi, lo


def _params(*sem):
    return pltpu.CompilerParams(dimension_semantics=sem, vmem_limit_bytes=VMEM_LIMIT)


def _ada_kernel(c_ref, w_ref, b_ref, o_ref):
    c = c_ref[...]
    cs = c / (1.0 + jnp.exp(-c))
    cs_hi, cs_lo = _split_bf16(cs)
    w_hi, w_lo = _split_bf16(w_ref[0])
    acc = _dot(cs_hi, w_hi) + _dot(cs_lo, w_hi) + _dot(cs_hi, w_lo)
    o_ref[0] = acc + b_ref[0]


def ada_mod(c_all, w_ada, b_ada):
    depth, d, n = w_ada.shape
    tn = 1024
    return pl.pallas_call(
        _ada_kernel,
        grid=(depth, n // tn),
        in_specs=[
            pl.BlockSpec((NB_PAD, d), lambda l, j: (0, 0)),
            pl.BlockSpec((1, d, tn), lambda l, j: (l, 0, j)),
            pl.BlockSpec((1, 1, tn), lambda l, j: (l, 0, j)),
        ],
        out_specs=pl.BlockSpec((1, NB_PAD, tn), lambda l, j: (l, 0, j)),
        out_shape=jax.ShapeDtypeStruct((depth, NB_PAD, n), F32),
        compiler_params=_params("parallel", "parallel"),
        name="ada_mod",
    )(c_all, w_ada, b_ada.reshape(depth, 1, n))


def _rms(x, g, eps):
    return x * lax.rsqrt(jnp.mean(x * x, axis=-1, keepdims=True) + eps) * g


def _norm_matmul_kernel(residual, tb_ref, *refs):
    if residual:
        x_ref, y_ref, g2_ref, sc_ref, sh_ref, g_ref, w_ref, o_ref, x2_ref, h_scr = refs
    else:
        x_ref, sc_ref, sh_ref, g_ref, w_ref, o_ref, h_scr = refs

    @pl.when(pl.program_id(1) == 0)
    def _():
        x = x_ref[...]
        if residual:
            x = x + g2_ref[0, 0] * y_ref[...]
            x2_ref[...] = x
        h = _rms(x, g_ref[...], EPS) * (1.0 + sc_ref[0, 0]) + sh_ref[0, 0]
        h_scr[...] = h.astype(BF16)

    o_ref[...] = _dot(h_scr[...], w_ref[...]).astype(o_ref.dtype)


def norm_matmul(x, tile_batch, mod4, sc_piece, sh_piece, g, w, y=None, g2_mod=None, g2_piece=None):
    t, d = x.shape
    n = w.shape[1]
    tm, tn = ROW_TILE, 512
    residual = y is not None
    row = pl.BlockSpec((tm, d), lambda i, j, tb: (i, 0))

    def piece(p):
        return pl.BlockSpec((1, 1, 1, d), lambda i, j, tb: (tb[i], p, 0, 0))

    in_specs = [row]
    args = [x]
    if residual:
        in_specs += [row, piece(g2_piece)]
        args += [y, g2_mod]
    in_specs += [piece(sc_piece), piece(sh_piece),
                 pl.BlockSpec((1, d), lambda i, j, tb: (0, 0)),
                 pl.BlockSpec((d, tn), lambda i, j, tb: (0, j))]
    args += [mod4, mod4, g.reshape(1, d), w]
    out_specs = [pl.BlockSpec((tm, tn), lambda i, j, tb: (i, j))]
    out_shape = [jax.ShapeDtypeStruct((t, n), BF16)]
    if residual:
        out_specs.append(row)
        out_shape.append(jax.ShapeDtypeStruct((t, d), F32))
    outs = pl.pallas_call(
        functools.partial(_norm_matmul_kernel, residual),
        grid_spec=pltpu.PrefetchScalarGridSpec(
            num_scalar_prefetch=1, grid=(t // tm, n // tn),
            in_specs=in_specs, out_specs=out_specs,
            scratch_shapes=[pltpu.VMEM((tm, d), BF16)]),
        out_shape=out_shape,
        compiler_params=_params("parallel", "arbitrary"),
        name="norm_matmul",
    )(tile_batch, *args)
    return outs if residual else (outs[0], x)


def _t5_bucket_np(rel):
    half = T5_BUCKETS // 2
    max_exact = half // 2
    sign = (rel > 0).astype(np.int64) * half
    n = np.abs(rel)
    nf = np.maximum(n, 1).astype(np.float64)
    large = max_exact + (np.log(nf / max_exact) / math.log(T5_MAX_DIST / max_exact)
                         * (half - max_exact)).astype(np.int64)
    large = np.minimum(large, half - 1)
    return (sign + np.where(n < max_exact, n, large)).astype(np.int32)


def _t5_band_tables(rel_bias, tq):
    w = tq + 2 * T5_BAND
    kk = np.arange(w)[None, :, None]
    qq = np.arange(tq)[None, None, :]
    delta = -T5_BAND * np.arange(3)[:, None, None]
    rel = kk + delta - qq
    bucket = _t5_bucket_np(rel)
    far_bucket = np.where(np.broadcast_to(kk + delta, rel.shape) < 0,
                          T5_BUCKETS // 2 - 1, T5_BUCKETS - 1).astype(np.int32)
    rb = rel_bias.astype(F32)
    tab = rb[bucket] - rb[far_bucket]
    return jnp.transpose(tab, (3, 0, 1, 2))


def _diffattn_kernel(seq, tq, lam_init, lam_ref, far_ref, q1_ref, q2_ref, k1_ref, k2_ref, v_ref,
                     band_ref, subg_ref, o_ref, vt_scr, s_scr):
    h = pl.program_id(1)
    qi = pl.program_id(2)
    w = tq + 2 * T5_BAND

    @pl.when(qi == 0)
    def _():
        vt_scr[...] = v_ref[...].astype(F32).T.astype(BF16)

    lane = lax.broadcasted_iota(I32, (1, LANES), 1)
    head_mask = jnp.where((lane // A_QK) == (h % 2), A_QK ** -0.5, 0.0).astype(BF16)
    q0 = qi * tq
    off = pl.multiple_of(jnp.clip(q0 - T5_BAND, 0, seq - w), LANES)
    var = (q0 - off) // T5_BAND
    key = lax.broadcasted_iota(I32, (seq, tq), 0)
    far = jnp.where(key < q0, far_ref[h, 0], far_ref[h, 1])

    def probs(q_ref, k_ref):
        s_scr[...] = _dot_nt(k_ref[...], q_ref[...] * head_mask) + far
        s_scr[pl.ds(off, w), :] += band_ref[0, var]
        s = s_scr[...]
        p = jnp.exp(s - jnp.max(s, axis=0, keepdims=True))
        return p, 1.0 / jnp.sum(p, axis=0, keepdims=True)

    p1, r1 = probs(q1_ref, k1_ref)
    p2, r2 = probs(q2_ref, k2_ref)
    a = (p1 * r1 - p2 * (lam_ref[0] * r2)).astype(BF16)
    ot = _dot(vt_scr[...], a)
    ms = jnp.mean(ot * ot, axis=0, keepdims=True)
    ot = ot * lax.rsqrt(ms + SUBLN_EPS) * (subg_ref[...] * (1.0 - lam_init))
    o_ref[...] = ot.T.astype(o_ref.dtype)


def diff_attention(proj, row_off, nbatch, seq, lam, far, band, subg_col, lam_init, tq):
    nq = seq // tq
    w = tq + 2 * T5_BAND
    assert seq >= w and row_off % seq == 0 and row_off % tq == 0
    boff = row_off // seq
    qoff = row_off // tq
    na = A_HEADS * A_QK // LANES

    def qspec(sec):
        return pl.BlockSpec((tq, LANES), lambda b, h, i: (qoff + b * nq + i, sec * na + h // 2))

    def kspec(sec):
        return pl.BlockSpec((seq, LANES), lambda b, h, i: (boff + b, sec * na + h // 2))

    smem = pl.BlockSpec(memory_space=pltpu.SMEM)
    return pl.pallas_call(
        functools.partial(_diffattn_kernel, seq, tq, lam_init),
        grid=(nbatch, A_HEADS, nq),
        in_specs=[smem, smem, qspec(0), qspec(1), kspec(2), kspec(3),
                  pl.BlockSpec((seq, A_V), lambda b, h, i: (boff + b, 4 * na + h)),
                  pl.BlockSpec((1, 3, w, tq), lambda b, h, i: (h, 0, 0, 0)),
                  pl.BlockSpec((A_V, 1), lambda b, h, i: (0, 0))],
        out_specs=pl.BlockSpec((tq, A_V), lambda b, h, i: (b * nq + i, h)),
        out_shape=jax.ShapeDtypeStruct((nbatch * seq, A_HEADS * A_V), BF16),
        scratch_shapes=[pltpu.VMEM((A_V, seq), BF16), pltpu.VMEM((seq, tq), F32)],
        compiler_params=_params("parallel", "parallel", "arbitrary"),
        name="diff_attention",
    )(lam, far, proj, proj, proj, proj, proj, band, subg_col)


def _na_tables(na_tab):
    v = np.arange(NA_WR)[:, None, None, None]
    c = np.arange(GRID_W)[None, :, None, None]
    kr = np.arange(NA_WR)[None, None, :, None]
    kc = np.arange(GRID_W)[None, None, None, :]
    cs = np.clip(c - NA_WC // 2, 0, GRID_W - NA_WC)
    shape = (NA_WR, GRID_W, NA_WR, GRID_W)
    valid = np.broadcast_to((kc >= cs) & (kc < cs + NA_WC), shape)
    dr = np.broadcast_to(kr - v + NA_WR - 1, shape)
    dc = np.broadcast_to(np.clip(kc - c + NA_WC - 1, 0, 2 * NA_WC - 2), shape)
    tab = na_tab.astype(F32)[:, dr, dc]
    tab = jnp.where(valid[None], tab, -1e30)
    return tab.reshape(B_HEADS, NA_WR, GRID_W, NA_WR * GRID_W)


def _na_kernel(rows, q_ref, k_ref, v_ref, tab_ref, o_ref):
    scale = B_HD ** -0.5
    nk = NA_WR * GRID_W

    def body(r, carry):
        rs = jnp.clip(r - NA_WR // 2, 0, rows - NA_WR)
        qs = pl.multiple_of(r * GRID_W, GRID_W)
        ks = pl.multiple_of(rs * GRID_W, GRID_W)
        q = q_ref[pl.ds(qs, GRID_W), :]
        s = _dot_nt(q, k_ref[pl.ds(ks, nk), :]) * scale + tab_ref[0, r - rs]
        p = jnp.exp(s - jnp.max(s, axis=-1, keepdims=True))
        p = (p / jnp.sum(p, axis=-1, keepdims=True)).astype(BF16)
        o_ref[pl.ds(qs, GRID_W), :] = _dot(p, v_ref[pl.ds(ks, nk), :]).astype(o_ref.dtype)
        return carry

    lax.fori_loop(0, rows, body, 0)


def neighbourhood_attention(proj, row_off, nbatch, seq, tab):
    rows = seq // GRID_W
    assert rows >= NA_WR and row_off % seq == 0
    boff = row_off // seq
    base = (4 * A_HEADS * A_QK + A_HEADS * A_V) // LANES
    nb = B_HEADS * B_HD // LANES

    def spec(sec):
        return pl.BlockSpec((seq, B_HD), lambda b, h: (boff + b, base + sec * nb + h))

    return pl.pallas_call(
        functools.partial(_na_kernel, rows),
        grid=(nbatch, B_HEADS),
        in_specs=[spec(0), spec(1), spec(2),
                  pl.BlockSpec((1, NA_WR, GRID_W, NA_WR * GRID_W), lambda b, h: (h, 0, 0, 0))],
        out_specs=pl.BlockSpec((seq, B_HD), lambda b, h: (b, h)),
        out_shape=jax.ShapeDtypeStruct((nbatch * seq, B_HEADS * B_HD), BF16),
        compiler_params=_params("parallel", "parallel"),
        name="neighbourhood_attention",
    )(proj, proj, proj, tab)


def _dft_channel_tables():
    j = np.arange(C_GD)
    ang = 2.0 * np.pi * ((j[:, None] * j[None, :]) % C_GD) / C_GD
    s = C_GD ** -0.5
    return jnp.asarray(np.cos(ang) * s, BF16), jnp.asarray(np.sin(ang) * s, BF16)


def _dft_position_table(seq):
    r = 64
    k = jnp.arange(seq, dtype=I32)
    jh = jnp.arange(seq // r, dtype=I32)
    jl = jnp.arange(r, dtype=I32)
    step = 2.0 * np.pi / seq
    a = ((jh[:, None] * r * k[None, :]) % seq).astype(F32) * step
    b = ((jl[:, None] * k[None, :]) % seq).astype(F32) * step
    ca, sa = jnp.cos(a)[:, None, :], jnp.sin(a)[:, None, :]
    cb, sb = jnp.cos(b)[None, :, :], jnp.sin(b)[None, :, :]
    scale = seq ** -0.5
    c = ((ca * cb - sa * sb) * scale).reshape(seq, seq)
    s = ((sa * cb + ca * sb) * -scale).reshape(seq, seq)
    return jnp.concatenate([c, s], axis=1).astype(BF16)


def _fourier_channel_kernel(u_ref, cc_ref, sc_ref, o_ref):
    for g in range(C_GROUPS):
        sl = slice(g * C_GD, (g + 1) * C_GD)
        ug = u_ref[:, sl]
        o_ref[0, 0, :, sl] = _dot(ug, cc_ref[...]).astype(o_ref.dtype)
        o_ref[0, 1, :, sl] = _dot(ug, sc_ref[...]).astype(o_ref.dtype)


def _matmul_kernel(a_ref, b_ref, o_ref):
    o_ref[...] = _dot(a_ref[...], b_ref[0]).astype(o_ref.dtype)


def fourier_mix(proj, row_off, nbatch, seq, cc, sc, pos_tab):
    tm = 512
    assert row_off % tm == 0 and seq % tm == 0
    ns = seq // tm
    roff = row_off // tm
    v = pl.pallas_call(
        _fourier_channel_kernel,
        grid=(nbatch, ns),
        in_specs=[pl.BlockSpec((tm, C_WIDTH), lambda b, i: (roff + b * ns + i, 0)),
                  pl.BlockSpec((C_GD, C_GD), lambda b, i: (0, 0)),
                  pl.BlockSpec((C_GD, C_GD), lambda b, i: (0, 0))],
        out_specs=pl.BlockSpec((1, 2, tm, C_WIDTH), lambda b, i: (b, 0, i, 0)),
        out_shape=jax.ShapeDtypeStruct((nbatch, 2, seq, C_WIDTH), BF16),
        compiler_params=_params("parallel", "parallel"),
        name="fourier_channels",
    )(proj, cc, sc)
    v = v.reshape(nbatch, 2 * seq, C_WIDTH)
    tn = 512
    return pl.pallas_call(
        _matmul_kernel,
        grid=(ns, nbatch, C_WIDTH // tn),
        in_specs=[pl.BlockSpec((tm, 2 * seq), lambda i, b, j: (i, 0)),
                  pl.BlockSpec((1, 2 * seq, tn), lambda i, b, j: (b, 0, j))],
        out_specs=pl.BlockSpec((tm, tn), lambda i, b, j: (b * ns + i, j)),
        out_shape=jax.ShapeDtypeStruct((nbatch * seq, C_WIDTH), BF16),
        compiler_params=_params("parallel", "parallel", "parallel"),
        name="fourier_positions",
    )(pos_tab, v)


def _conv_kernel(bg_ref, cg_ref, hv_ref, cw_ref, cb_ref, o_ref):
    u = cg_ref[...].astype(F32) * hv_ref[...].astype(F32)
    seq = u.shape[0]
    row = lax.broadcasted_iota(I32, u.shape, 0)
    prev = jnp.where(row == 0, 0.0, pltpu.roll(u, 1, 0))
    nxt = jnp.where(row == seq - 1, 0.0, pltpu.roll(u, seq - 1, 0))
    y = prev * cw_ref[0:1, :] + u * cw_ref[1:2, :] + nxt * cw_ref[2:3, :] + cb_ref[...]
    o_ref[...] = (bg_ref[...].astype(F32) * y).astype(o_ref.dtype)


def short_conv(proj, row_off, nbatch, seq, cw, cb):
    tc = 256
    assert row_off % seq == 0
    boff = row_off // seq
    nct = D_WIDTH // tc
    base = C_WIDTH // tc

    def spec(sec):
        return pl.BlockSpec((seq, tc), lambda b, j: (boff + b, base + sec * nct + j))

    return pl.pallas_call(
        _conv_kernel,
        grid=(nbatch, nct),
        in_specs=[spec(0), spec(1), spec(2),
                  pl.BlockSpec((3, tc), lambda b, j: (0, j)),
                  pl.BlockSpec((1, tc), lambda b, j: (0, j))],
        out_specs=pl.BlockSpec((seq, tc), lambda b, j: (b, j)),
        out_shape=jax.ShapeDtypeStruct((nbatch * seq, D_WIDTH), BF16),
        compiler_params=_params("parallel", "parallel"),
        name="short_conv",
    )(proj, proj, proj, cw, cb.reshape(1, D_WIDTH))


def _route(logits, bias):
    s = 1.0 / (1.0 + jnp.exp(-logits))
    sel = s + bias
    sv = [s[j * N_GROUPS:(j + 1) * N_GROUPS] for j in range(EPG)]
    cv = [sel[j * N_GROUPS:(j + 1) * N_GROUPS] for j in range(EPG)]
    hi1, lo1 = jnp.maximum(cv[0], cv[1]), jnp.minimum(cv[0], cv[1])
    hi2, lo2 = jnp.maximum(cv[2], cv[3]), jnp.minimum(cv[2], cv[3])
    gscore = jnp.maximum(hi1, hi2) + jnp.maximum(jnp.minimum(hi1, hi2), jnp.maximum(lo1, lo2))
    gidx = lax.broadcasted_iota(I32, gscore.shape, 0).astype(F32)
    gmax = jnp.max(gscore, axis=0, keepdims=True)
    best = jnp.min(jnp.where(gscore == gmax, gidx, float(N_GROUPS)), axis=0, keepdims=True)
    onehot = gidx == best

    def pick(z):
        return jnp.sum(jnp.where(onehot, z, 0.0), axis=0, keepdims=True)

    c = [pick(z) for z in cv]
    w = [pick(z) for z in sv]
    i1, v1 = jnp.zeros_like(best), c[0]
    for j in range(1, EPG):
        better = c[j] > v1
        i1 = jnp.where(better, float(j), i1)
        v1 = jnp.where(better, c[j], v1)
    i2, v2 = jnp.full_like(best, -1.0), jnp.full_like(best, -jnp.inf)
    for j in range(EPG):
        better = (i1 != float(j)) & ((c[j] > v2) | (i2 < 0.0))
        i2 = jnp.where(better, float(j), i2)
        v2 = jnp.where(better, c[j], v2)
    lo = jnp.minimum(i1, i2)
    hi = jnp.maximum(i1, i2)

    def gate(idx):
        g = w[0]
        for j in range(1, EPG):
            g = jnp.where(idx == float(j), w[j], g)
        return g

    w_lo, w_hi = gate(lo), gate(hi)
    total = w_lo + w_hi
    pair = jnp.where(lo == 0.0, 0.0, jnp.where(lo == 1.0, 3.0, 5.0)) + (hi - lo - 1.0)
    cls = (best * float(N_PAIRS) + pair).astype(I32)
    return cls, w_lo / total, w_hi / total


def _outproj_router_kernel(n1, tb_ref, a1_ref, b1_ref, a2_ref, b2_ref, x_ref, g1_ref, sc_ref, sh_ref,
                           g_ref, w_ref, wrh_ref, wrl_ref, br_ref,
                           xo_ref, h_ref, cls_ref, wlo_ref, whi_ref):
    first = pl.program_id(0) < n1
    o1 = jnp.where(first, a1_ref[...], a2_ref[...])
    o2 = jnp.where(first, b1_ref[...], b2_ref[...])
    m = _dot(o1, w_ref[0]) + _dot(o2, w_ref[1])
    x = x_ref[...] + g1_ref[0, 0] * m
    xo_ref[...] = x
    h = _rms(x, g_ref[...], EPS) * (1.0 + sc_ref[0, 0]) + sh_ref[0, 0]
    h_ref[...] = h
    h_hi, h_lo = _split_bf16(h)
    logits = _dot_nt(wrh_ref[...], h_hi) + _dot_nt(wrh_ref[...], h_lo) + _dot_nt(wrl_ref[...], h_hi)
    cls, w_lo, w_hi = _route(logits, br_ref[...])
    cls_ref[...] = cls
    wlo_ref[...] = w_lo
    whi_ref[...] = w_hi


def outproj_router(o1, o2, n_first, x, tile_batch, mod4, w_out, g, wr_hi, wr_lo, br):
    t, d = x.shape
    tm = ROUTE_TILE
    assert n_first % tm == 0
    n1 = n_first // tm
    half = w_out.shape[0] // 2

    def first(i, tb):
        return (jnp.minimum(i, n1 - 1), 0)

    def second(i, tb):
        return (jnp.maximum(i - n1, 0), 0)

    def piece(p):
        return pl.BlockSpec((1, 1, 1, d), lambda i, tb: (tb[i], p, 0, 0))

    row = pl.BlockSpec((tm, d), lambda i, tb: (i, 0))
    vec = pl.BlockSpec((1, tm), lambda i, tb: (0, i))
    full = lambda shape: pl.BlockSpec(shape, lambda i, tb: (0,) * len(shape))
    return pl.pallas_call(
        functools.partial(_outproj_router_kernel, n1),
        grid_spec=pltpu.PrefetchScalarGridSpec(
            num_scalar_prefetch=1, grid=(t // tm,),
            in_specs=[pl.BlockSpec((tm, half), first), pl.BlockSpec((tm, half), first),
                      pl.BlockSpec((tm, half), second), pl.BlockSpec((tm, half), second),
                      row, piece(2), piece(4), piece(3), full((1, d)),
                      full((2, half, d)), full((N_EXPERTS, d)), full((N_EXPERTS, d)),
                      full((N_EXPERTS, 1))],
            out_specs=[row, row, vec, vec, vec]),
        out_shape=[jax.ShapeDtypeStruct((t, d), F32), jax.ShapeDtypeStruct((t, d), F32),
                   jax.ShapeDtypeStruct((1, t), I32), jax.ShapeDtypeStruct((1, t), F32),
                   jax.ShapeDtypeStruct((1, t), F32)],
        compiler_params=_params("parallel"),
        name="outproj_router",
    )(tile_batch, o1[0], o2[0], o1[1], o2[1], x, mod4, mod4, mod4, g.reshape(1, d),
      w_out.reshape(2, half, d), wr_hi, wr_lo, br)


def _gather_kernel(bm, idx_ref, src_ref, o_ref, sem):
    def start(r, carry):
        pltpu.make_async_copy(src_ref.at[pl.ds(idx_ref[0, 0, r], 1), :],
                              o_ref.at[pl.ds(r, 1), :], sem).start()
        return carry

    def wait(r, carry):
        pltpu.make_async_copy(src_ref.at[pl.ds(0, 1), :], o_ref.at[pl.ds(r, 1), :], sem).wait()
        return carry

    lax.fori_loop(0, bm, start, 0)
    lax.fori_loop(0, bm, wait, 0)


def gather_rows(src, idx, bm=256):
    n = idx.shape[0]
    d = src.shape[1]
    assert n % bm == 0
    return pl.pallas_call(
        functools.partial(_gather_kernel, bm),
        grid=(n // bm,),
        in_specs=[pl.BlockSpec((1, 1, bm), lambda i: (i, 0, 0), memory_space=pltpu.SMEM),
                  pl.BlockSpec(memory_space=pl.ANY)],
        out_specs=pl.BlockSpec((bm, d), lambda i: (i, 0)),
        out_shape=jax.ShapeDtypeStruct((n, d), src.dtype),
        scratch_shapes=[pltpu.SemaphoreType.DMA(())],
        compiler_params=_params("arbitrary"),
        name="gather_rows",
    )(idx.reshape(n // bm, 1, bm), src)


def _moe_kernel(ea_ref, eb_ref, nused_ref, x_ref, wa_ref, wb_ref, gua_ref, dna_ref, gub_ref, dnb_ref,
                o_ref):
    @pl.when(pl.program_id(0) < nused_ref[0])
    def _():
        x = x_ref[...].astype(BF16)

        def expert(gu_ref, dn_ref):
            gu = _dot(x, gu_ref[0])
            g, u = gu[:, :D_EXPERT], gu[:, D_EXPERT:]
            act = (g / (1.0 + jnp.exp(-g))) * u
            return _dot(act.astype(BF16), dn_ref[0])

        o_ref[...] = wa_ref[...] * expert(gua_ref, dna_ref) + wb_ref[...] * expert(gub_ref, dnb_ref)

    @pl.when(pl.program_id(0) >= nused_ref[0])
    def _():
        o_ref[...] = jnp.zeros_like(o_ref)


def moe_experts(xs, blk_a, blk_b, n_used, row_wa, row_wb, w_gu, w_dn):
    nrows, d = xs.shape
    bm = MOE_BLOCK
    row = pl.BlockSpec((bm, d), lambda i, ea, eb, nu: (i, 0))
    col = pl.BlockSpec((bm, 1), lambda i, ea, eb, nu: (i, 0))
    gu = lambda sel: pl.BlockSpec((1, d, 2 * D_EXPERT),
                                  lambda i, ea, eb, nu: ((ea, eb)[sel][i], 0, 0))
    dn = lambda sel: pl.BlockSpec((1, D_EXPERT, d),
                                  lambda i, ea, eb, nu: ((ea, eb)[sel][i], 0, 0))
    return pl.pallas_call(
        _moe_kernel,
        grid_spec=pltpu.PrefetchScalarGridSpec(
            num_scalar_prefetch=3, grid=(nrows // bm,),
            in_specs=[row, col, col, gu(0), dn(0), gu(1), dn(1)],
            out_specs=row),
        out_shape=jax.ShapeDtypeStruct((nrows, d), F32),
        compiler_params=_params("arbitrary"),
        name="moe_experts",
    )(blk_a, blk_b, n_used, xs, row_wa.reshape(nrows, 1), row_wb.reshape(nrows, 1),
      w_gu, w_dn, w_gu, w_dn)


def _dispatch(cls, w_lo, w_hi):
    t = cls.shape[0]
    bm = MOE_BLOCK
    nblk = -(-t // bm) + N_CLASSES
    nrows = nblk * bm
    order = jnp.argsort(cls).astype(I32)
    cs = cls[order]
    counts = jnp.zeros((N_CLASSES,), I32).at[cls].add(1)
    padded = (counts + bm - 1) // bm * bm
    pend = jnp.cumsum(padded)
    pstart = pend - padded
    start = jnp.cumsum(counts) - counts
    dest = pstart[cs] + jnp.arange(t, dtype=I32) - start[cs]
    row_tok = jnp.zeros((nrows,), I32).at[dest].set(order)
    row_wa = jnp.zeros((nrows,), F32).at[dest].set(w_lo[order])
    row_wb = jnp.zeros((nrows,), F32).at[dest].set(w_hi[order])
    pos = jnp.zeros((t,), I32).at[order].set(dest)
    blk_start = jnp.arange(nblk, dtype=I32) * bm
    blk_cls = jnp.minimum(jnp.sum((pend[None, :] <= blk_start[:, None]).astype(I32), axis=-1),
                          N_CLASSES - 1)
    grp = blk_cls // N_PAIRS
    pair = blk_cls % N_PAIRS
    blk_a = grp * EPG + jnp.asarray(PAIR_LO)[pair]
    blk_b = grp * EPG + jnp.asarray(PAIR_HI)[pair]
    n_used = (pend[-1] // bm).reshape(1).astype(I32)
    return row_tok, pos, row_wa, row_wb, blk_a.astype(I32), blk_b.astype(I32), n_used


def _final_kernel(tb_ref, x_ref, y_ref, g2_ref, g_ref, o_ref):
    x = x_ref[...] + g2_ref[0, 0] * y_ref[...]
    o_ref[...] = _rms(x, g_ref[...], EPS)


def final_norm(x, y, tile_batch, tile_off, nrows, mod4, g):
    d = x.shape[1]
    tm = ROW_TILE
    row = pl.BlockSpec((tm, d), lambda i, tb: (tile_off + i, 0))
    return pl.pallas_call(
        _final_kernel,
        grid_spec=pltpu.PrefetchScalarGridSpec(
            num_scalar_prefetch=1, grid=(nrows // tm,),
            in_specs=[row, row,
                      pl.BlockSpec((1, 1, 1, d), lambda i, tb: (tb[tile_off + i], 5, 0, 0)),
                      pl.BlockSpec((1, d), lambda i, tb: (0, 0))],
            out_specs=pl.BlockSpec((tm, d), lambda i, tb: (i, 0))),
        out_shape=jax.ShapeDtypeStruct((nrows, d), F32),
        compiler_params=_params("parallel"),
        name="final_norm",
    )(tile_batch, x, y, mod4, g.reshape(1, d))


def _tile_batch(tile, groups):
    out = []
    base = 0
    for nb, seq in groups:
        assert seq % tile == 0
        out.append(base + np.repeat(np.arange(nb), seq // tile))
        base += nb
    return jnp.asarray(np.concatenate(out), I32)


def kernel(x_prompt, x_sample, c_prompt, c_sample, w_ada, b_ada, norm_mix_g, norm_ffn_g, w_in_ab, w_out_ab, lambda_q1, lambda_k1, lambda_q2, lambda_k2, subln_g, na_bias, w_in_cd, conv_w, conv_b, w_out_cd, rel_bias, w_router, b_router, w_gate_up, w_down, final_norm_g):
    depth, d = norm_mix_g.shape
    groups = [(x_prompt.shape[0], x_prompt.shape[1]), (x_sample.shape[0], x_sample.shape[1])]
    rows = [nb * seq for nb, seq in groups]
    offs = [0, rows[0]]
    t = rows[0] + rows[1]
    nb_all = groups[0][0] + groups[1][0]
    assert nb_all <= NB_PAD

    x = jnp.concatenate([x_prompt.reshape(rows[0], d), x_sample.reshape(rows[1], d)], axis=0)
    c_all = jnp.concatenate([c_prompt, c_sample, jnp.zeros((NB_PAD - nb_all, d), F32)], axis=0)
    mod = ada_mod(c_all, w_ada, b_ada).reshape(depth, NB_PAD, 6, 1, d)
    tb_row = _tile_batch(ROW_TILE, groups)
    tb_route = _tile_batch(ROUTE_TILE, groups)

    wr = w_router.astype(F32).T.reshape(N_GROUPS, EPG, d).transpose(1, 0, 2).reshape(N_EXPERTS, d)
    wr_hi, wr_lo = _split_bf16(wr)
    br = b_router.astype(F32).reshape(N_GROUPS, EPG).T.reshape(N_EXPERTS, 1)

    cc, sc = _dft_channel_tables()
    y = None
    for l in range(depth):
        i = l // 2
        mod4 = mod[l]
        if l % 2 == 0:
            w_in, w_out = w_in_ab[i], w_out_ab[i]
        else:
            w_in, w_out = w_in_cd[i], w_out_cd[i]
        proj, x = norm_matmul(x, tb_row, mod4, 1, 0, norm_mix_g[l], w_in.astype(BF16),
                              y=y, g2_mod=mod[l - 1] if l else None, g2_piece=5)
        if l % 2 == 0:
            lam_init = 0.8 - 0.6 * math.exp(-0.3 * l)
            lam = (jnp.exp(jnp.sum(lambda_q1[i].astype(F32) * lambda_k1[i].astype(F32)))
                   - jnp.exp(jnp.sum(lambda_q2[i].astype(F32) * lambda_k2[i].astype(F32)))
                   + lam_init).reshape(1)
            rb = rel_bias.astype(F32)
            far = jnp.stack([rb[T5_BUCKETS // 2 - 1], rb[T5_BUCKETS - 1]], axis=1)
            na_tab = _na_tables(na_bias[i])
            subg = subln_g[i].astype(F32).reshape(A_V, 1)
            o1, o2 = [], []
            for (nb, seq), off in zip(groups, offs):
                tq = min(ATT_TQ, seq - 2 * T5_BAND)
                band = _t5_band_tables(rel_bias, tq)
                o1.append(diff_attention(proj, off, nb, seq, lam, far, band, subg, lam_init, tq))
                o2.append(neighbourhood_attention(proj, off, nb, seq, na_tab))
        else:
            o1, o2 = [], []
            for (nb, seq), off in zip(groups, offs):
                o1.append(fourier_mix(proj, off, nb, seq, cc, sc, _dft_position_table(seq)))
                o2.append(short_conv(proj, off, nb, seq, conv_w[i].astype(F32), conv_b[i].astype(F32)))
        x, h2, cls, w_lo, w_hi = outproj_router(o1, o2, rows[0], x, tb_route, mod4,
                                                w_out.astype(BF16), norm_ffn_g[l], wr_hi, wr_lo, br)
        row_tok, pos, row_wa, row_wb, blk_a, blk_b, n_used = _dispatch(cls[0], w_lo[0], w_hi[0])
        xs = gather_rows(h2, row_tok)
        ys = moe_experts(xs, blk_a, blk_b, n_used, row_wa, row_wb,
                         w_gate_up[l].astype(BF16), w_down[l].astype(BF16))
        y = gather_rows(ys, pos)

    mod4 = mod[depth - 1]
    outs = []
    for (nb, seq), off, n in zip(groups, offs, rows):
        o = final_norm(x, y, tb_row, off // ROW_TILE, n, mod4, final_norm_g)
        outs.append(o.reshape(nb, seq, d))
    return tuple(outs)
```

```python
import functools
import math

import numpy as np
import jax
import jax.numpy as jnp
from jax import lax
from jax.experimental import pallas as pl
from jax.experimental.pallas import tpu as pltpu

F32 = jnp.float32
BF16 = jnp.bfloat16
I32 = jnp.int32

GRID_W = 64
A_HEADS = 8
A_QK = 64
A_V = 128
SUBLN_EPS = 1e-5
B_HEADS = 8
B_HD = 128
NA_WR = 8
NA_WC = 16
C_GROUPS = 4
C_GD = 256
C_WIDTH = 1024
D_WIDTH = 1024
T5_BUCKETS = 32
T5_MAX_DIST = 128
N_EXPERTS = 32
N_GROUPS = 8
EPG = 4
D_EXPERT = 512
EPS = 1e-6
N_PAIRS = 6
N_CLASSES = N_GROUPS * N_PAIRS
PAIR_LO = np.array([0, 0, 0, 1, 1, 2], np.int32)
PAIR_HI = np.array([1, 2, 3, 2, 3, 3], np.int32)

LANES = 128
VMEM_LIMIT = 56 * 1024 * 1024

NB_PAD = 16
ROW_TILE = 512
PROJ_TILE = 256
ROUTE_TILE = 256
MOE_BLOCK = 256
ATT_TQ = 256
DMA_UNROLL = 8


def _dot(a, b):
    return jnp.dot(a, b, preferred_element_type=F32)


def _dot_nt(a, b):
    return lax.dot_general(a, b, (((1,), (1,)), ((), ())), preferred_element_type=F32)


def _split_bf16(x):
    hi = x.astype(BF16)
    lo = (x - hi.astype(F32)).astype(BF16)
    return hi, lo


def _params(*sem):
    return pltpu.CompilerParams(dimension_semantics=sem, vmem_limit_bytes=VMEM_LIMIT)


def _ada_kernel(c_ref, w_ref, b_ref, o_ref):
    c = c_ref[...]
    cs = c / (1.0 + jnp.exp(-c))
    cs_hi, cs_lo = _split_bf16(cs)
    w_hi, w_lo = _split_bf16(w_ref[0])
    acc = _dot(cs_hi, w_hi) + _dot(cs_lo, w_hi) + _dot(cs_hi, w_lo)
    o_ref[0] = acc + b_ref[0]


def ada_mod(c_all, w_ada, b_ada):
    depth, d, n = w_ada.shape
    tn = 1024
    return pl.pallas_call(
        _ada_kernel,
        grid=(depth, n // tn),
        in_specs=[
            pl.BlockSpec((NB_PAD, d), lambda l, j: (0, 0)),
            pl.BlockSpec((1, d, tn), lambda l, j: (l, 0, j)),
            pl.BlockSpec((1, 1, tn), lambda l, j: (l, 0, j)),
        ],
        out_specs=pl.BlockSpec((1, NB_PAD, tn), lambda l, j: (l, 0, j)),
        out_shape=jax.ShapeDtypeStruct((depth, NB_PAD, n), F32),
        compiler_params=_params("parallel", "parallel"),
        name="ada_mod",
    )(c_all, w_ada, b_ada.reshape(depth, 1, n))


def _rms(x, g, eps):
    return x * lax.rsqrt(jnp.mean(x * x, axis=-1, keepdims=True) + eps) * g


def _norm_matmul_kernel(residual, tn, tb_ref, *refs):
    if residual:
        x_ref, y_ref, g2_ref, sc_ref, sh_ref, g_ref, w_ref, o_ref, x2_ref, h_scr = refs
    else:
        x_ref, sc_ref, sh_ref, g_ref, w_ref, o_ref, h_scr = refs
    x = x_ref[...]
    if residual:
        x = x + g2_ref[0, 0] * y_ref[...]
        x2_ref[...] = x
    h = _rms(x, g_ref[...], EPS) * (1.0 + sc_ref[0, 0]) + sh_ref[0, 0]
    h_scr[...] = h.astype(BF16)
    for j in range(w_ref.shape[1] // tn):
        cols = slice(j * tn, (j + 1) * tn)
        o_ref[:, cols] = _dot(h_scr[...], w_ref[:, cols]).astype(o_ref.dtype)


def norm_matmul(x, tile_batch, mod4, sc_piece, sh_piece, g, w, y=None, g2_mod=None, g2_piece=None):
    t, d = x.shape
    n = w.shape[1]
    tm, tn = PROJ_TILE, 1024
    residual = y is not None
    row = pl.BlockSpec((tm, d), lambda i, tb: (i, 0))

    def piece(p):
        return pl.BlockSpec((1, 1, 1, d), lambda i, tb: (tb[i], p, 0, 0))

    in_specs = [row]
    args = [x]
    if residual:
        in_specs += [row, piece(g2_piece)]
        args += [y, g2_mod]
    in_specs += [piece(sc_piece), piece(sh_piece),
                 pl.BlockSpec((1, d), lambda i, tb: (0, 0)),
                 pl.BlockSpec((d, n), lambda i, tb: (0, 0), pipeline_mode=pl.Buffered(1))]
    args += [mod4, mod4, g.reshape(1, d), w]
    out_specs = [pl.BlockSpec((tm, n), lambda i, tb: (i, 0))]
    out_shape = [jax.ShapeDtypeStruct((t, n), BF16)]
    if residual:
        out_specs.append(row)
        out_shape.append(jax.ShapeDtypeStruct((t, d), F32))
    outs = pl.pallas_call(
        functools.partial(_norm_matmul_kernel, residual, tn),
        grid_spec=pltpu.PrefetchScalarGridSpec(
            num_scalar_prefetch=1, grid=(t // tm,),
            in_specs=in_specs, out_specs=out_specs,
            scratch_shapes=[pltpu.VMEM((tm, d), BF16)]),
        out_shape=out_shape,
        compiler_params=_params("parallel"),
        name="norm_matmul",
    )(tile_batch, *args)
    return outs if residual else (outs[0], x)


def _t5_bucket_np(rel):
    half = T5_BUCKETS // 2
    max_exact = half // 2
    sign = (rel > 0).astype(np.int64) * half
    n = np.abs(rel)
    nf = np.maximum(n, 1).astype(np.float64)
    large = max_exact + (np.log(nf / max_exact) / math.log(T5_MAX_DIST / max_exact)
                         * (half - max_exact)).astype(np.int64)
    large = np.minimum(large, half - 1)
    return (sign + np.where(n < max_exact, n, large)).astype(np.int32)


def _t5_band_tables(rel_bias, tq):
    d = np.arange(-1, 2)[:, None, None]
    kk = np.arange(tq)[None, :, None]
    qq = np.arange(tq)[None, None, :]
    bucket = jnp.asarray(_t5_bucket_np(d * tq + kk - qq)[None].astype(np.int8))
    rb = rel_bias.astype(F32).T[:, :, None, None, None]
    tab = jnp.zeros((A_HEADS, 3, tq, tq), F32)
    for b in range(T5_BUCKETS):
        tab = jnp.where(bucket == b, rb[:, b], tab)
    return tab


def _diffattn_kernel(seq, tq, lam_init, lam_ref, far_ref, q1_ref, q2_ref, k1_ref, k2_ref, v_ref,
                     band_ref, subg_ref, o_ref, vt_scr, s1_scr, s2_scr, p1_scr, p2_scr):
    h = pl.program_id(1)
    qi = pl.program_id(2)
    nch = seq // tq
    grp = tq // 8

    @pl.when(qi == 0)
    def _():
        vt_scr[...] = v_ref[...].astype(F32).T.astype(BF16)

    lane = lax.broadcasted_iota(I32, (1, LANES), 1)
    head_mask = jnp.where((lane // A_QK) == (h % 2), A_QK ** -0.5, 0.0).astype(BF16)
    q1 = q1_ref[...] * head_mask
    q2 = q2_ref[...] * head_mask
    far_l, far_r = far_ref[h, 0], far_ref[h, 1]

    def fold(x):
        return x.reshape(grp, 8, tq)

    def scores(c, carry):
        m1, m2 = carry
        ks = pl.multiple_of(c * tq, tq)
        d = c - qi
        near = band_ref[0, jnp.clip(d + 1, 0, 2)]
        bias = jnp.where(jnp.abs(d) <= 1, near, jnp.where(d < 0, far_l, far_r))
        s1 = _dot_nt(k1_ref[pl.ds(ks, tq), :], q1) + bias
        s2 = _dot_nt(k2_ref[pl.ds(ks, tq), :], q2) + bias
        s1_scr[pl.ds(ks, tq), :] = s1
        s2_scr[pl.ds(ks, tq), :] = s2
        return (jnp.maximum(m1, jnp.max(fold(s1), axis=0)),
                jnp.maximum(m2, jnp.max(fold(s2), axis=0)))

    neg = jnp.full((8, tq), -jnp.inf, F32)
    m1, m2 = lax.fori_loop(0, nch, scores, (neg, neg), unroll=min(nch, 8))
    m1 = jnp.max(m1, axis=0, keepdims=True)
    m2 = jnp.max(m2, axis=0, keepdims=True)

    def exps(c, carry):
        l1, l2 = carry
        ks = pl.multiple_of(c * tq, tq)
        p1 = jnp.exp(s1_scr[pl.ds(ks, tq), :] - m1)
        p2 = jnp.exp(s2_scr[pl.ds(ks, tq), :] - m2)
        p1_scr[pl.ds(ks, tq), :] = p1.astype(BF16)
        p2_scr[pl.ds(ks, tq), :] = p2.astype(BF16)
        return l1 + jnp.sum(fold(p1), axis=0), l2 + jnp.sum(fold(p2), axis=0)

    zero = jnp.zeros((8, tq), F32)
    l1, l2 = lax.fori_loop(0, nch, exps, (zero, zero), unroll=4)
    n1 = 1.0 / jnp.sum(l1, axis=0, keepdims=True)
    n2 = lam_ref[0] / jnp.sum(l2, axis=0, keepdims=True)
    ot = _dot(vt_scr[...], p1_scr[...]) * n1 - _dot(vt_scr[...], p2_scr[...]) * n2
    ms = jnp.mean(ot * ot, axis=0, keepdims=True)
    ot = ot * lax.rsqrt(ms + SUBLN_EPS) * (subg_ref[...] * (1.0 - lam_init))
    o_ref[...] = ot.T.astype(o_ref.dtype)


def diff_attention(proj, row_off, nbatch, seq, lam, far, band, subg_col, lam_init, tq):
    nq = seq // tq
    assert tq >= LANES and nq >= 3 and row_off % seq == 0
    boff = row_off // seq
    qoff = row_off // tq
    na = A_HEADS * A_QK // LANES

    def qspec(sec):
        return pl.BlockSpec((tq, LANES), lambda b, h, i: (qoff + b * nq + i, sec * na + h // 2))

    def kspec(sec):
        return pl.BlockSpec((seq, LANES), lambda b, h, i: (boff + b, sec * na + h // 2))

    smem = pl.BlockSpec(memory_space=pltpu.SMEM)
    return pl.pallas_call(
        functools.partial(_diffattn_kernel, seq, tq, lam_init),
        grid=(nbatch, A_HEADS, nq),
        in_specs=[smem, smem, qspec(0), qspec(1), kspec(2), kspec(3),
                  pl.BlockSpec((seq, A_V), lambda b, h, i: (boff + b, 4 * na + h)),
                  pl.BlockSpec((1, 3, tq, tq), lambda b, h, i: (h, 0, 0, 0)),
                  pl.BlockSpec((A_V, 1), lambda b, h, i: (0, 0))],
        out_specs=pl.BlockSpec((tq, A_V), lambda b, h, i: (b * nq + i, h)),
        out_shape=jax.ShapeDtypeStruct((nbatch * seq, A_HEADS * A_V), BF16),
        scratch_shapes=[pltpu.VMEM((A_V, seq), BF16), pltpu.VMEM((seq, tq), F32),
                        pltpu.VMEM((seq, tq), F32), pltpu.VMEM((seq, tq), BF16),
                        pltpu.VMEM((seq, tq), BF16)],
        compiler_params=_params("parallel", "parallel", "arbitrary"),
        name="diff_attention",
    )(lam, far, proj, proj, proj, proj, proj, band, subg_col)


def _na_tables(na_tab):
    v = np.arange(NA_WR)[:, None]
    kr = np.arange(NA_WR)[None, :]
    dr = jnp.asarray((kr - v + NA_WR - 1).astype(np.int8))[None, :, :, None]
    c = np.arange(GRID_W)[:, None]
    kc = np.arange(GRID_W)[None, :]
    cs = np.clip(c - NA_WC // 2, 0, GRID_W - NA_WC)
    valid = (kc >= cs) & (kc < cs + NA_WC)
    dc = np.where(valid, kc - c + NA_WC - 1, -1).astype(np.int8)
    dc = jnp.asarray(dc)[None, None, :, None, :]
    na = na_tab.astype(F32)
    rows = jnp.zeros((B_HEADS, NA_WR, NA_WR, 2 * NA_WC - 1), F32)
    for r in range(2 * NA_WR - 1):
        rows = jnp.where(dr == r, na[:, r][:, None, None, :], rows)
    tab = jnp.full((B_HEADS, NA_WR, GRID_W, NA_WR, GRID_W), -1e30, F32)
    for s in range(2 * NA_WC - 1):
        tab = jnp.where(dc == s, rows[:, :, None, :, s][..., None], tab)
    return tab.reshape(B_HEADS, NA_WR, GRID_W, NA_WR * GRID_W)


def _na_kernel(rows, q_ref, k_ref, v_ref, tab_ref, o_ref):
    scale = B_HD ** -0.5
    nk = NA_WR * GRID_W

    def body(r, carry):
        rs = jnp.clip(r - NA_WR // 2, 0, rows - NA_WR)
        qs = pl.multiple_of(r * GRID_W, GRID_W)
        ks = pl.multiple_of(rs * GRID_W, GRID_W)
        q = q_ref[pl.ds(qs, GRID_W), :]
        s = _dot_nt(q, k_ref[pl.ds(ks, nk), :]) * scale + tab_ref[0, r - rs]
        p = jnp.exp(s - jnp.max(s, axis=-1, keepdims=True))
        p = (p / jnp.sum(p, axis=-1, keepdims=True)).astype(BF16)
        o_ref[pl.ds(qs, GRID_W), :] = _dot(p, v_ref[pl.ds(ks, nk), :]).astype(o_ref.dtype)
        return carry

    lax.fori_loop(0, rows, body, 0, unroll=8)


def neighbourhood_attention(proj, row_off, nbatch, seq, tab):
    rows = seq // GRID_W
    assert rows >= NA_WR and row_off % seq == 0
    boff = row_off // seq
    base = (4 * A_HEADS * A_QK + A_HEADS * A_V) // LANES
    nb = B_HEADS * B_HD // LANES

    def spec(sec):
        return pl.BlockSpec((seq, B_HD), lambda b, h: (boff + b, base + sec * nb + h))

    return pl.pallas_call(
        functools.partial(_na_kernel, rows),
        grid=(nbatch, B_HEADS),
        in_specs=[spec(0), spec(1), spec(2),
                  pl.BlockSpec((1, NA_WR, GRID_W, NA_WR * GRID_W), lambda b, h: (h, 0, 0, 0))],
        out_specs=pl.BlockSpec((seq, B_HD), lambda b, h: (b, h)),
        out_shape=jax.ShapeDtypeStruct((nbatch * seq, B_HEADS * B_HD), BF16),
        compiler_params=_params("parallel", "parallel"),
        name="neighbourhood_attention",
    )(proj, proj, proj, tab)


def _dft_channel_tables():
    j = np.arange(C_GD)
    ang = 2.0 * np.pi * ((j[:, None] * j[None, :]) % C_GD) / C_GD
    s = C_GD ** -0.5
    return jnp.asarray(np.cos(ang) * s, BF16), jnp.asarray(np.sin(ang) * s, BF16)


def _dft_position_table(seq):
    r = 64
    k = jnp.arange(seq, dtype=I32)
    jh = jnp.arange(seq // r, dtype=I32)
    jl = jnp.arange(r, dtype=I32)
    step = 2.0 * np.pi / seq
    a = ((jh[:, None] * r * k[None, :]) % seq).astype(F32) * step
    b = ((jl[:, None] * k[None, :]) % seq).astype(F32) * step
    ca, sa = jnp.cos(a)[:, None, :], jnp.sin(a)[:, None, :]
    cb, sb = jnp.cos(b)[None, :, :], jnp.sin(b)[None, :, :]
    scale = seq ** -0.5
    c = ((ca * cb - sa * sb) * scale).reshape(seq, seq)
    s = ((sa * cb + ca * sb) * -scale).reshape(seq, seq)
    return jnp.concatenate([c, s], axis=1).astype(BF16)


def _fourier_channel_kernel(u_ref, cc_ref, sc_ref, o_ref):
    for g in range(C_GROUPS):
        sl = slice(g * C_GD, (g + 1) * C_GD)
        ug = u_ref[:, sl]
        o_ref[0, 0, :, sl] = _dot(ug, cc_ref[...]).astype(o_ref.dtype)
        o_ref[0, 1, :, sl] = _dot(ug, sc_ref[...]).astype(o_ref.dtype)


def _matmul_kernel(a_ref, b_ref, o_ref):
    o_ref[...] = _dot(a_ref[...], b_ref[0]).astype(o_ref.dtype)


def fourier_mix(proj, row_off, nbatch, seq, cc, sc, pos_tab):
    tm = 512
    assert row_off % tm == 0 and seq % tm == 0
    ns = seq // tm
    roff = row_off // tm
    v = pl.pallas_call(
        _fourier_channel_kernel,
        grid=(nbatch, ns),
        in_specs=[pl.BlockSpec((tm, C_WIDTH), lambda b, i: (roff + b * ns + i, 0)),
                  pl.BlockSpec((C_GD, C_GD), lambda b, i: (0, 0)),
                  pl.BlockSpec((C_GD, C_GD), lambda b, i: (0, 0))],
        out_specs=pl.BlockSpec((1, 2, tm, C_WIDTH), lambda b, i: (b, 0, i, 0)),
        out_shape=jax.ShapeDtypeStruct((nbatch, 2, seq, C_WIDTH), BF16),
        compiler_params=_params("parallel", "parallel"),
        name="fourier_channels",
    )(proj, cc, sc)
    v = v.reshape(nbatch, 2 * seq, C_WIDTH)
    tn = 512
    return pl.pallas_call(
        _matmul_kernel,
        grid=(ns, nbatch, C_WIDTH // tn),
        in_specs=[pl.BlockSpec((tm, 2 * seq), lambda i, b, j: (i, 0)),
                  pl.BlockSpec((1, 2 * seq, tn), lambda i, b, j: (b, 0, j))],
        out_specs=pl.BlockSpec((tm, tn), lambda i, b, j: (b * ns + i, j)),
        out_shape=jax.ShapeDtypeStruct((nbatch * seq, C_WIDTH), BF16),
        compiler_params=_params("parallel", "parallel", "parallel"),
        name="fourier_positions",
    )(pos_tab, v)


def _conv_kernel(bg_ref, cg_ref, hv_ref, cw_ref, cb_ref, o_ref):
    u = cg_ref[...].astype(F32) * hv_ref[...].astype(F32)
    seq = u.shape[0]
    row = lax.broadcasted_iota(I32, u.shape, 0)
    prev = jnp.where(row == 0, 0.0, pltpu.roll(u, 1, 0))
    nxt = jnp.where(row == seq - 1, 0.0, pltpu.roll(u, seq - 1, 0))
    y = prev * cw_ref[0:1, :] + u * cw_ref[1:2, :] + nxt * cw_ref[2:3, :] + cb_ref[...]
    o_ref[...] = (bg_ref[...].astype(F32) * y).astype(o_ref.dtype)


def short_conv(proj, row_off, nbatch, seq, cw, cb):
    tc = 256
    assert row_off % seq == 0
    boff = row_off // seq
    nct = D_WIDTH // tc
    base = C_WIDTH // tc

    def spec(sec):
        return pl.BlockSpec((seq, tc), lambda b, j: (boff + b, base + sec * nct + j))

    return pl.pallas_call(
        _conv_kernel,
        grid=(nbatch, nct),
        in_specs=[spec(0), spec(1), spec(2),
                  pl.BlockSpec((3, tc), lambda b, j: (0, j)),
                  pl.BlockSpec((1, tc), lambda b, j: (0, j))],
        out_specs=pl.BlockSpec((seq, tc), lambda b, j: (b, j)),
        out_shape=jax.ShapeDtypeStruct((nbatch * seq, D_WIDTH), BF16),
        compiler_params=_params("parallel", "parallel"),
        name="short_conv",
    )(proj, proj, proj, cw, cb.reshape(1, D_WIDTH))


def _route(logits, bias):
    s = 1.0 / (1.0 + jnp.exp(-logits))
    sel = s + bias
    sv = [s[j * N_GROUPS:(j + 1) * N_GROUPS] for j in range(EPG)]
    cv = [sel[j * N_GROUPS:(j + 1) * N_GROUPS] for j in range(EPG)]
    hi1, lo1 = jnp.maximum(cv[0], cv[1]), jnp.minimum(cv[0], cv[1])
    hi2, lo2 = jnp.maximum(cv[2], cv[3]), jnp.minimum(cv[2], cv[3])
    gscore = jnp.maximum(hi1, hi2) + jnp.maximum(jnp.minimum(hi1, hi2), jnp.maximum(lo1, lo2))
    gidx = lax.broadcasted_iota(I32, gscore.shape, 0).astype(F32)
    gmax = jnp.max(gscore, axis=0, keepdims=True)
    best = jnp.min(jnp.where(gscore == gmax, gidx, float(N_GROUPS)), axis=0, keepdims=True)
    onehot = gidx == best

    def pick(z):
        return jnp.sum(jnp.where(onehot, z, 0.0), axis=0, keepdims=True)

    c = [pick(z) for z in cv]
    w = [pick(z) for z in sv]
    i1, v1 = jnp.zeros_like(best), c[0]
    for j in range(1, EPG):
        better = c[j] > v1
        i1 = jnp.where(better, float(j), i1)
        v1 = jnp.where(better, c[j], v1)
    i2, v2 = jnp.full_like(best, -1.0), jnp.full_like(best, -jnp.inf)
    for j in range(EPG):
        better = (i1 != float(j)) & ((c[j] > v2) | (i2 < 0.0))
        i2 = jnp.where(better, float(j), i2)
        v2 = jnp.where(better, c[j], v2)
    lo = jnp.minimum(i1, i2)
    hi = jnp.maximum(i1, i2)

    def gate(idx):
        g = w[0]
        for j in range(1, EPG):
            g = jnp.where(idx == float(j), w[j], g)
        return g

    w_lo, w_hi = gate(lo), gate(hi)
    total = w_lo + w_hi
    pair = jnp.where(lo == 0.0, 0.0, jnp.where(lo == 1.0, 3.0, 5.0)) + (hi - lo - 1.0)
    cls = (best * float(N_PAIRS) + pair).astype(I32)
    return cls, w_lo / total, w_hi / total


def _outproj_router_kernel(n1, tb_ref, a1_ref, b1_ref, a2_ref, b2_ref, x_ref, g1_ref, sc_ref, sh_ref,
                           g_ref, w_ref, wrh_ref, wrl_ref, br_ref,
                           xo_ref, h_ref, cls_ref, wlo_ref, whi_ref):
    first = pl.program_id(0) < n1
    o1 = jnp.where(first, a1_ref[...], a2_ref[...])
    o2 = jnp.where(first, b1_ref[...], b2_ref[...])
    m = _dot(o1, w_ref[0]) + _dot(o2, w_ref[1])
    x = x_ref[...] + g1_ref[0, 0] * m
    xo_ref[...] = x
    h = _rms(x, g_ref[...], EPS) * (1.0 + sc_ref[0, 0]) + sh_ref[0, 0]
    h_ref[...] = h
    h_hi, h_lo = _split_bf16(h)
    logits = _dot(h_hi, wrh_ref[...]) + _dot(h_lo, wrh_ref[...]) + _dot(h_hi, wrl_ref[...])
    cls, w_lo, w_hi = _route(logits.T[:N_EXPERTS], br_ref[...])
    cls_ref[...] = cls
    wlo_ref[...] = w_lo
    whi_ref[...] = w_hi


def outproj_router(o1, o2, n_first, x, tile_batch, mod4, w_out, g, wr_hi, wr_lo, br):
    t, d = x.shape
    tm = ROUTE_TILE
    assert n_first % tm == 0
    n1 = n_first // tm
    half = w_out.shape[0] // 2

    def first(i, tb):
        return (jnp.minimum(i, n1 - 1), 0)

    def second(i, tb):
        return (jnp.maximum(i - n1, 0), 0)

    def piece(p):
        return pl.BlockSpec((1, 1, 1, d), lambda i, tb: (tb[i], p, 0, 0))

    row = pl.BlockSpec((tm, d), lambda i, tb: (i, 0))
    vec = pl.BlockSpec((1, tm), lambda i, tb: (0, i))
    full = lambda shape: pl.BlockSpec(shape, lambda i, tb: (0,) * len(shape))
    return pl.pallas_call(
        functools.partial(_outproj_router_kernel, n1),
        grid_spec=pltpu.PrefetchScalarGridSpec(
            num_scalar_prefetch=1, grid=(t // tm,),
            in_specs=[pl.BlockSpec((tm, half), first), pl.BlockSpec((tm, half), first),
                      pl.BlockSpec((tm, half), second), pl.BlockSpec((tm, half), second),
                      row, piece(2), piece(4), piece(3), full((1, d)),
                      full((2, half, d)), full((d, LANES)), full((d, LANES)),
                      full((N_EXPERTS, 1))],
            out_specs=[row, row, vec, vec, vec]),
        out_shape=[jax.ShapeDtypeStruct((t, d), F32), jax.ShapeDtypeStruct((t, d), F32),
                   jax.ShapeDtypeStruct((1, t), I32), jax.ShapeDtypeStruct((1, t), F32),
                   jax.ShapeDtypeStruct((1, t), F32)],
        compiler_params=_params("parallel"),
        name="outproj_router",
    )(tile_batch, o1[0], o2[0], o1[1], o2[1], x, mod4, mod4, mod4, g.reshape(1, d),
      w_out.reshape(2, half, d), wr_hi, wr_lo, br)


def _moe_kernel(bm, nsteps, ea_ref, eb_ref, nused_ref, nvalid_ref,
                tok_first_ref, tok_next_ref, tok_ref, h_ref, wa_ref, wb_ref,
                gua_ref, dna_ref, gub_ref, dnb_ref, y_ref, xbuf, ybuf, gsem, ssem):
    i = pl.program_id(0)
    n = nused_ref[0]
    slot = i % 2

    def gather(ids_ref, s):
        def body(c, carry):
            for j in range(DMA_UNROLL):
                r = c * DMA_UNROLL + j
                pltpu.make_async_copy(h_ref.at[pl.ds(ids_ref[0, 0, r], 1), :],
                                      xbuf.at[s, pl.ds(r, 1), :], gsem.at[s]).start()
            return carry

        lax.fori_loop(0, bm // DMA_UNROLL, body, 0)

    def wait_gather(s):
        def body(c, carry):
            for _ in range(DMA_UNROLL):
                pltpu.make_async_copy(h_ref.at[pl.ds(0, 1), :], xbuf.at[s, pl.ds(0, 1), :],
                                      gsem.at[s]).wait()
            return carry

        lax.fori_loop(0, bm // DMA_UNROLL, body, 0)

    def wait_scatter(s, count):
        def one():
            pltpu.make_async_copy(ybuf.at[s, pl.ds(0, 1), :], y_ref.at[pl.ds(0, 1), :],
                                  ssem.at[s]).wait()

        def many(c, carry):
            for _ in range(DMA_UNROLL):
                one()
            return carry

        def single(r, carry):
            one()
            return carry

        lax.fori_loop(0, count // DMA_UNROLL, many, 0)
        lax.fori_loop(0, count % DMA_UNROLL, single, 0)

    @pl.when(i == 0)
    def _():
        gather(tok_first_ref, 0)

    @pl.when(i + 1 < n)
    def _():
        gather(tok_next_ref, 1 - slot)

    @pl.when(i < n)
    def _():
        wait_gather(slot)

        @pl.when(i >= 2)
        def _():
            wait_scatter(slot, nvalid_ref[i - 2])

        x = xbuf[slot].astype(BF16)

        def expert(gu_ref, dn_ref):
            gu = _dot(x, gu_ref[0])
            g, u = gu[:, :D_EXPERT], gu[:, D_EXPERT:]
            act = (g / (1.0 + jnp.exp(-g))) * u
            return _dot(act.astype(BF16), dn_ref[0])

        ybuf[slot] = wa_ref[...] * expert(gua_ref, dna_ref) + wb_ref[...] * expert(gub_ref, dnb_ref)

        def put(r):
            pltpu.make_async_copy(ybuf.at[slot, pl.ds(r, 1), :],
                                  y_ref.at[pl.ds(tok_ref[0, 0, r], 1), :], ssem.at[slot]).start()

        def many(c, carry):
            for j in range(DMA_UNROLL):
                put(c * DMA_UNROLL + j)
            return carry

        def single(r, carry):
            put(r)
            return carry

        nv = nvalid_ref[i]
        full = nv // DMA_UNROLL
        lax.fori_loop(0, full, many, 0)
        lax.fori_loop(full * DMA_UNROLL, nv, single, 0)

    @pl.when(i == nsteps - 1)
    def _():
        wait_scatter((n - 1) % 2, nvalid_ref[n - 1])

        @pl.when(n >= 2)
        def _():
            wait_scatter(n % 2, nvalid_ref[jnp.maximum(n - 2, 0)])


def moe_experts(h, row_tok, blk_a, blk_b, n_used, n_valid, row_wa, row_wb, w_gu, w_dn):
    t, d = h.shape
    bm = MOE_BLOCK
    nrows = row_tok.shape[0]
    nblk = nrows // bm
    col = pl.BlockSpec((bm, 1), lambda i, *_: (i, 0))
    ids = lambda f: pl.BlockSpec((1, 1, bm), lambda i, *_: (f(i), 0, 0), memory_space=pltpu.SMEM)
    gu = lambda sel: pl.BlockSpec((1, d, 2 * D_EXPERT), lambda i, *s: (s[sel][i], 0, 0))
    dn = lambda sel: pl.BlockSpec((1, D_EXPERT, d), lambda i, *s: (s[sel][i], 0, 0))
    hbm = pl.BlockSpec(memory_space=pl.ANY)
    tok3 = row_tok.reshape(nblk, 1, bm)
    return pl.pallas_call(
        functools.partial(_moe_kernel, bm, nblk),
        grid_spec=pltpu.PrefetchScalarGridSpec(
            num_scalar_prefetch=4, grid=(nblk,),
            in_specs=[ids(lambda i: 0), ids(lambda i: jnp.minimum(i + 1, nblk - 1)), ids(lambda i: i),
                      hbm, col, col, gu(0), dn(0), gu(1), dn(1)],
            out_specs=hbm,
            scratch_shapes=[pltpu.VMEM((2, bm, d), F32), pltpu.VMEM((2, bm, d), F32),
                            pltpu.SemaphoreType.DMA((2,)), pltpu.SemaphoreType.DMA((2,))]),
        out_shape=jax.ShapeDtypeStruct((t, d), F32),
        compiler_params=_params("arbitrary"),
        name="moe_experts",
    )(blk_a, blk_b, n_used, n_valid, tok3, tok3, tok3, h,
      row_wa.reshape(nrows, 1), row_wb.reshape(nrows, 1), w_gu, w_dn, w_gu, w_dn)


def _dispatch(cls, w_lo, w_hi):
    t = cls.shape[0]
    bm = MOE_BLOCK
    nblk = -(-t // bm) + N_CLASSES
    nrows = nblk * bm
    iota = jnp.arange(t, dtype=I32)
    _, order, s_lo, s_hi = lax.sort((cls, iota, w_lo, w_hi), num_keys=1, is_stable=True)
    classes = jnp.arange(N_CLASSES, dtype=I32)
    counts = jnp.sum((cls[:, None] == classes[None, :]).astype(I32), axis=0)
    padded = (counts + bm - 1) // bm * bm
    pend = jnp.cumsum(padded)
    pstart = pend - padded
    start = jnp.cumsum(counts) - counts
    blk_start = jnp.arange(nblk, dtype=I32) * bm
    blk_cls = jnp.minimum(jnp.sum((pend[None, :] <= blk_start[:, None]).astype(I32), axis=-1),
                          N_CLASSES - 1)
    k0 = blk_start - pstart[blk_cls]
    k = k0[:, None] + jnp.arange(bm, dtype=I32)[None, :]
    valid = (k < counts[blk_cls][:, None]).reshape(nrows)
    src = jnp.clip(start[blk_cls][:, None] + k, 0, t - 1).reshape(nrows)
    row_tok = jnp.where(valid, order[src], 0)
    row_wa = jnp.where(valid, s_lo[src], 0.0)
    row_wb = jnp.where(valid, s_hi[src], 0.0)
    n_valid = jnp.clip(counts[blk_cls] - k0, 0, bm).astype(I32)
    grp = blk_cls // N_PAIRS
    pair = blk_cls % N_PAIRS
    blk_a = grp * EPG + jnp.asarray(PAIR_LO)[pair]
    blk_b = grp * EPG + jnp.asarray(PAIR_HI)[pair]
    n_used = (pend[-1] // bm).reshape(1).astype(I32)
    return row_tok, row_wa, row_wb, blk_a.astype(I32), blk_b.astype(I32), n_used, n_valid


def _final_kernel(tb_ref, x_ref, y_ref, g2_ref, g_ref, o_ref):
    x = x_ref[...] + g2_ref[0, 0] * y_ref[...]
    o_ref[...] = _rms(x, g_ref[...], EPS)


def final_norm(x, y, tile_batch, tile_off, nrows, mod4, g):
    d = x.shape[1]
    tm = ROW_TILE
    row = pl.BlockSpec((tm, d), lambda i, tb: (tile_off + i, 0))
    return pl.pallas_call(
        _final_kernel,
        grid_spec=pltpu.PrefetchScalarGridSpec(
            num_scalar_prefetch=1, grid=(nrows // tm,),
            in_specs=[row, row,
                      pl.BlockSpec((1, 1, 1, d), lambda i, tb: (tb[tile_off + i], 5, 0, 0)),
                      pl.BlockSpec((1, d), lambda i, tb: (0, 0))],
            out_specs=pl.BlockSpec((tm, d), lambda i, tb: (i, 0))),
        out_shape=jax.ShapeDtypeStruct((nrows, d), F32),
        compiler_params=_params("parallel"),
        name="final_norm",
    )(tile_batch, x, y, mod4, g.reshape(1, d))


def _tile_batch(tile, groups):
    out = []
    base = 0
    for nb, seq in groups:
        assert seq % tile == 0
        out.append(base + np.repeat(np.arange(nb), seq // tile))
        base += nb
    return jnp.asarray(np.concatenate(out), I32)


def kernel(x_prompt, x_sample, c_prompt, c_sample, w_ada, b_ada, norm_mix_g, norm_ffn_g, w_in_ab, w_out_ab, lambda_q1, lambda_k1, lambda_q2, lambda_k2, subln_g, na_bias, w_in_cd, conv_w, conv_b, w_out_cd, rel_bias, w_router, b_router, w_gate_up, w_down, final_norm_g):
    depth, d = norm_mix_g.shape
    groups = [(x_prompt.shape[0], x_prompt.shape[1]), (x_sample.shape[0], x_sample.shape[1])]
    rows = [nb * seq for nb, seq in groups]
    offs = [0, rows[0]]
    t = rows[0] + rows[1]
    nb_all = groups[0][0] + groups[1][0]
    assert nb_all <= NB_PAD

    x = jnp.concatenate([x_prompt.reshape(rows[0], d), x_sample.reshape(rows[1], d)], axis=0)
    c_all = jnp.concatenate([c_prompt, c_sample, jnp.zeros((NB_PAD - nb_all, d), F32)], axis=0)
    mod = ada_mod(c_all, w_ada, b_ada).reshape(depth, NB_PAD, 6, 1, d)
    tb_row = _tile_batch(ROW_TILE, groups)
    tb_route = _tile_batch(ROUTE_TILE, groups)
    tb_proj = _tile_batch(PROJ_TILE, groups)

    wr = w_router.astype(F32).reshape(d, N_GROUPS, EPG).transpose(0, 2, 1).reshape(d, N_EXPERTS)
    wr_hi, wr_lo = _split_bf16(jnp.pad(wr, ((0, 0), (0, LANES - N_EXPERTS))))
    br = b_router.astype(F32).reshape(N_GROUPS, EPG).T.reshape(N_EXPERTS, 1)

    cc, sc = _dft_channel_tables()
    y = None
    for l in range(depth):
        i = l // 2
        mod4 = mod[l]
        if l % 2 == 0:
            w_in, w_out = w_in_ab[i], w_out_ab[i]
        else:
            w_in, w_out = w_in_cd[i], w_out_cd[i]
        proj, x = norm_matmul(x, tb_proj, mod4, 1, 0, norm_mix_g[l], w_in.astype(BF16),
                              y=y, g2_mod=mod[l - 1] if l else None, g2_piece=5)
        if l % 2 == 0:
            lam_init = 0.8 - 0.6 * math.exp(-0.3 * l)
            lam = (jnp.exp(jnp.sum(lambda_q1[i].astype(F32) * lambda_k1[i].astype(F32)))
                   - jnp.exp(jnp.sum(lambda_q2[i].astype(F32) * lambda_k2[i].astype(F32)))
                   + lam_init).reshape(1)
            rb = rel_bias.astype(F32)
            far = jnp.stack([rb[T5_BUCKETS // 2 - 1], rb[T5_BUCKETS - 1]], axis=1)
            na_tab = _na_tables(na_bias[i])
            subg = subln_g[i].astype(F32).reshape(A_V, 1)
            o1, o2, bands = [], [], {}
            for (nb, seq), off in zip(groups, offs):
                tq = min(ATT_TQ, seq // 4)
                if tq not in bands:
                    bands[tq] = _t5_band_tables(rel_bias, tq)
                o1.append(diff_attention(proj, off, nb, seq, lam, far, bands[tq], subg, lam_init, tq))
                o2.append(neighbourhood_attention(proj, off, nb, seq, na_tab))
        else:
            o1, o2 = [], []
            for (nb, seq), off in zip(groups, offs):
                o1.append(fourier_mix(proj, off, nb, seq, cc, sc, _dft_position_table(seq)))
                o2.append(short_conv(proj, off, nb, seq, conv_w[i].astype(F32), conv_b[i].astype(F32)))
        x, h2, cls, w_lo, w_hi = outproj_router(o1, o2, rows[0], x, tb_route, mod4,
                                                w_out.astype(BF16), norm_ffn_g[l], wr_hi, wr_lo, br)
        row_tok, row_wa, row_wb, blk_a, blk_b, n_used, n_valid = _dispatch(cls[0], w_lo[0], w_hi[0])
        y = moe_experts(h2, row_tok, blk_a, blk_b, n_used, n_valid, row_wa, row_wb,
                        w_gate_up[l].astype(BF16), w_down[l].astype(BF16))

    mod4 = mod[depth - 1]
    outs = []
    for (nb, seq), off, n in zip(groups, offs, rows):
        o = final_norm(x, y, tb_row, off // ROW_TILE, n, mod4, final_norm_g)
        outs.append(o.reshape(nb, seq, d))
    return tuple(outs)
```

```python
import functools
import math

import numpy as np
import jax
import jax.numpy as jnp
from jax import lax
from jax.experimental import pallas as pl
from jax.experimental.pallas import tpu as pltpu

F32 = jnp.float32
BF16 = jnp.bfloat16
I32 = jnp.int32

GRID_W = 64
A_HEADS = 8
A_QK = 64
A_V = 128
SUBLN_EPS = 1e-5
B_HEADS = 8
B_HD = 128
NA_WR = 8
NA_WC = 16
NA_QR = 8
NA_KR = 16
C_GROUPS = 4
C_GD = 256
C_WIDTH = 1024
D_WIDTH = 1024
T5_BUCKETS = 32
T5_MAX_DIST = 128
N_EXPERTS = 32
N_GROUPS = 8
EPG = 4
D_EXPERT = 512
EPS = 1e-6
N_PAIRS = 6
N_CLASSES = N_GROUPS * N_PAIRS
PAIR_LO = np.array([0, 0, 0, 1, 1, 2], np.int32)
PAIR_HI = np.array([1, 2, 3, 2, 3, 3], np.int32)

LANES = 128
VMEM_LIMIT = 56 * 1024 * 1024

NB_PAD = 16
ROW_TILE = 512
PROJ_TILE = 256
ROUTE_TILE = 256
MOE_BLOCK = 256
ATT_TQ = 256
DMA_UNROLL = 8
SUM_ROWS = 16


def _dot(a, b):
    return jnp.dot(a, b, preferred_element_type=F32)


def _dot_nt(a, b):
    return lax.dot_general(a, b, (((1,), (1,)), ((), ())), preferred_element_type=F32)


def _split_bf16(x):
    hi = x.astype(BF16)
    lo = (x - hi.astype(F32)).astype(BF16)
    return hi, lo


def _params(*sem):
    return pltpu.CompilerParams(dimension_semantics=sem, vmem_limit_bytes=VMEM_LIMIT)


def _ada_kernel(c_ref, w_ref, b_ref, o_ref):
    c = c_ref[...]
    cs = c / (1.0 + jnp.exp(-c))
    cs_hi, cs_lo = _split_bf16(cs)
    w_hi, w_lo = _split_bf16(w_ref[0])
    acc = _dot(cs_hi, w_hi) + _dot(cs_lo, w_hi) + _dot(cs_hi, w_lo)
    o_ref[0] = acc + b_ref[0]


def ada_mod(c_all, w_ada, b_ada):
    depth, d, n = w_ada.shape
    tn = 1024
    return pl.pallas_call(
        _ada_kernel,
        grid=(depth, n // tn),
        in_specs=[
            pl.BlockSpec((NB_PAD, d), lambda l, j: (0, 0)),
            pl.BlockSpec((1, d, tn), lambda l, j: (l, 0, j)),
            pl.BlockSpec((1, 1, tn), lambda l, j: (l, 0, j)),
        ],
        out_specs=pl.BlockSpec((1, NB_PAD, tn), lambda l, j: (l, 0, j)),
        out_shape=jax.ShapeDtypeStruct((depth, NB_PAD, n), F32),
        compiler_params=_params("parallel", "parallel"),
        name="ada_mod",
    )(c_all, w_ada, b_ada.reshape(depth, 1, n))


def _rms(x, g, eps):
    return x * lax.rsqrt(jnp.mean(x * x, axis=-1, keepdims=True) + eps) * g


def _norm_matmul_kernel(n_first, tn, tb_ref, *refs):
    if n_first is None:
        x_ref, y_ref, g2_ref, sc_ref, sh_ref, g_ref, w_ref, o_ref, x2_ref, h_scr = refs
        x = x_ref[...] + g2_ref[0, 0] * y_ref[...]
    else:
        xa_ref, xb_ref, sc_ref, sh_ref, g_ref, w_ref, o_ref, x2_ref, h_scr = refs
        x = jnp.where(pl.program_id(0) < n_first, xa_ref[...], xb_ref[...])
    x2_ref[...] = x
    h = _rms(x, g_ref[...], EPS) * (1.0 + sc_ref[0, 0]) + sh_ref[0, 0]
    h_scr[...] = h.astype(BF16)
    for j in range(w_ref.shape[1] // tn):
        cols = slice(j * tn, (j + 1) * tn)
        o_ref[:, cols] = _dot(h_scr[...], w_ref[:, cols]).astype(o_ref.dtype)


def norm_matmul(x, tile_batch, mod4, sc_piece, sh_piece, g, w, y=None, g2_mod=None, g2_piece=None):
    n = w.shape[1]
    tm, tn = PROJ_TILE, 1024

    def piece(p):
        return pl.BlockSpec((1, 1, 1, d), lambda i, tb: (tb[i], p, 0, 0))

    if y is None:
        xa, xb = x
        d = xa.shape[1]
        t = xa.shape[0] + xb.shape[0]
        assert xa.shape[0] % tm == 0
        n_first = xa.shape[0] // tm
        in_specs = [pl.BlockSpec((tm, d), lambda i, tb: (jnp.minimum(i, n_first - 1), 0)),
                    pl.BlockSpec((tm, d), lambda i, tb: (jnp.maximum(i - n_first, 0), 0))]
        args = [xa, xb]
    else:
        t, d = x.shape
        n_first = None
        in_specs = [pl.BlockSpec((tm, d), lambda i, tb: (i, 0)),
                    pl.BlockSpec((tm, d), lambda i, tb: (i, 0)), piece(g2_piece)]
        args = [x, y, g2_mod]
    in_specs += [piece(sc_piece), piece(sh_piece),
                 pl.BlockSpec((1, d), lambda i, tb: (0, 0)),
                 pl.BlockSpec((d, n), lambda i, tb: (0, 0), pipeline_mode=pl.Buffered(1))]
    args += [mod4, mod4, g.reshape(1, d), w]
    return pl.pallas_call(
        functools.partial(_norm_matmul_kernel, n_first, tn),
        grid_spec=pltpu.PrefetchScalarGridSpec(
            num_scalar_prefetch=1, grid=(t // tm,),
            in_specs=in_specs,
            out_specs=[pl.BlockSpec((tm, n), lambda i, tb: (i, 0)),
                       pl.BlockSpec((tm, d), lambda i, tb: (i, 0))],
            scratch_shapes=[pltpu.VMEM((tm, d), BF16)]),
        out_shape=[jax.ShapeDtypeStruct((t, n), BF16), jax.ShapeDtypeStruct((t, d), F32)],
        compiler_params=_params("parallel"),
        name="norm_matmul",
    )(tile_batch, *args)


def _t5_bucket_np(rel):
    half = T5_BUCKETS // 2
    max_exact = half // 2
    sign = (rel > 0).astype(np.int64) * half
    n = np.abs(rel)
    nf = np.maximum(n, 1).astype(np.float64)
    large = max_exact + (np.log(nf / max_exact) / math.log(T5_MAX_DIST / max_exact)
                         * (half - max_exact)).astype(np.int64)
    large = np.minimum(large, half - 1)
    return (sign + np.where(n < max_exact, n, large)).astype(np.int32)


def _t5_band_tables(rel_bias, tq):
    d = np.arange(-1, 2)[:, None, None]
    kk = np.arange(tq)[None, :, None]
    qq = np.arange(tq)[None, None, :]
    bucket = jnp.asarray(_t5_bucket_np(d * tq + kk - qq)[None].astype(np.int8))
    rb = rel_bias.astype(F32).T[:, :, None, None, None]
    tab = jnp.zeros((A_HEADS, 3, tq, tq), F32)
    for b in range(T5_BUCKETS):
        tab = jnp.where(bucket == b, rb[:, b], tab)
    return tab


def _diffattn_kernel(seq, tq, lam_init, lam_ref, far_ref, q1_ref, q2_ref, k1_ref, k2_ref, v_ref,
                     band_ref, subg_ref, o_ref, vt_scr, s1_scr, s2_scr, p1_scr, p2_scr):
    h = pl.program_id(1)
    qi = pl.program_id(2)
    nch = seq // tq
    grp = tq // 8
    ngrp, _, gkeys = vt_scr.shape

    @pl.when(qi == 0)
    def _():
        for g in range(ngrp):
            vt_scr[g, :A_V, :] = v_ref[g * gkeys:(g + 1) * gkeys, :].astype(F32).T.astype(BF16)
            vt_scr[g, A_V:, :] = jnp.ones((SUM_ROWS, gkeys), BF16)

    lane = lax.broadcasted_iota(I32, (1, LANES), 1)
    head_mask = jnp.where((lane // A_QK) == (h % 2), A_QK ** -0.5, 0.0).astype(BF16)
    q1 = q1_ref[...] * head_mask
    q2 = q2_ref[...] * head_mask
    far_l, far_r = far_ref[h, 0], far_ref[h, 1]

    def fold(x):
        return x.reshape(grp, 8, tq)

    def scores(c, carry):
        m1, m2 = carry
        ks = pl.multiple_of(c * tq, tq)
        d = c - qi
        near = band_ref[0, jnp.clip(d + 1, 0, 2)]
        bias = jnp.where(jnp.abs(d) <= 1, near, jnp.where(d < 0, far_l, far_r))
        s1 = _dot_nt(k1_ref[pl.ds(ks, tq), :], q1) + bias
        s2 = _dot_nt(k2_ref[pl.ds(ks, tq), :], q2) + bias
        s1_scr[pl.ds(ks, tq), :] = s1
        s2_scr[pl.ds(ks, tq), :] = s2
        return (jnp.maximum(m1, jnp.max(fold(s1), axis=0)),
                jnp.maximum(m2, jnp.max(fold(s2), axis=0)))

    neg = jnp.full((8, tq), -jnp.inf, F32)
    m1, m2 = lax.fori_loop(0, nch, scores, (neg, neg), unroll=min(nch, 8))
    m1 = jnp.max(m1, axis=0, keepdims=True)
    m2 = jnp.max(m2, axis=0, keepdims=True)

    acc1 = jnp.zeros((A_V + SUM_ROWS, tq), F32)
    acc2 = jnp.zeros((A_V + SUM_ROWS, tq), F32)
    for g in range(ngrp):
        rows = slice(g * gkeys, (g + 1) * gkeys)
        p1_scr[rows, :] = jnp.exp((s1_scr[rows, :] - m1).astype(BF16))
        p2_scr[rows, :] = jnp.exp((s2_scr[rows, :] - m2).astype(BF16))
        acc1 = acc1 + _dot(vt_scr[g], p1_scr[rows, :])
        acc2 = acc2 + _dot(vt_scr[g], p2_scr[rows, :])
    n1 = 1.0 / acc1[A_V:A_V + 1, :]
    n2 = lam_ref[0] / acc2[A_V:A_V + 1, :]
    ot = acc1[:A_V, :] * n1 - acc2[:A_V, :] * n2
    ms = jnp.mean(ot * ot, axis=0, keepdims=True)
    ot = ot * lax.rsqrt(ms + SUBLN_EPS) * (subg_ref[...] * (1.0 - lam_init))
    o_ref[...] = ot.T.astype(o_ref.dtype)


def diff_attention(proj, row_off, nbatch, seq, lam, far, band, subg_col, lam_init, tq):
    nq = seq // tq
    gch = 4
    assert tq >= LANES and nq >= 3 and nq % gch == 0 and row_off % seq == 0
    boff = row_off // seq
    qoff = row_off // tq
    na = A_HEADS * A_QK // LANES

    def qspec(sec):
        return pl.BlockSpec((tq, LANES), lambda b, h, i: (qoff + b * nq + i, sec * na + h // 2))

    def kspec(sec):
        return pl.BlockSpec((seq, LANES), lambda b, h, i: (boff + b, sec * na + h // 2))

    smem = pl.BlockSpec(memory_space=pltpu.SMEM)
    return pl.pallas_call(
        functools.partial(_diffattn_kernel, seq, tq, lam_init),
        grid=(nbatch, A_HEADS, nq),
        in_specs=[smem, smem, qspec(0), qspec(1), kspec(2), kspec(3),
                  pl.BlockSpec((seq, A_V), lambda b, h, i: (boff + b, 4 * na + h)),
                  pl.BlockSpec((1, 3, tq, tq), lambda b, h, i: (h, 0, 0, 0)),
                  pl.BlockSpec((A_V, 1), lambda b, h, i: (0, 0))],
        out_specs=pl.BlockSpec((tq, A_V), lambda b, h, i: (b * nq + i, h)),
        out_shape=jax.ShapeDtypeStruct((nbatch * seq, A_HEADS * A_V), BF16),
        scratch_shapes=[pltpu.VMEM((nq // gch, A_V + SUM_ROWS, gch * tq), BF16),
                        pltpu.VMEM((seq, tq), F32), pltpu.VMEM((seq, tq), F32),
                        pltpu.VMEM((seq, tq), BF16), pltpu.VMEM((seq, tq), BF16)],
        compiler_params=_params("parallel", "parallel", "arbitrary"),
        name="diff_attention",
    )(lam, far, proj, proj, proj, proj, proj, band, subg_col)


def _na_tables(na_tab):
    half = NA_WR // 2
    qr = np.arange(NA_QR)[:, None]
    kr = np.arange(NA_KR)[None, :]
    dr = np.full((3, NA_QR, NA_KR), -1, np.int64)
    for var, (rs, off) in enumerate([(np.maximum(qr - half, 0), NA_WR - 1),
                                     (qr, half - 1),
                                     (half + np.minimum(qr, half), -1)]):
        dr[var] = np.where((kr >= rs) & (kr < rs + NA_WR), kr - qr + off, -1)
    dr = jnp.asarray(dr.astype(np.int8))[None, :, :, :, None]
    c = np.arange(GRID_W)[:, None]
    kc = np.arange(GRID_W)[None, :]
    cs = np.clip(c - NA_WC // 2, 0, GRID_W - NA_WC)
    dc = np.where((kc >= cs) & (kc < cs + NA_WC), kc - c + NA_WC - 1, -1).astype(np.int8)
    dc = jnp.asarray(dc)[None, None, None, :, None, :]
    na = na_tab.astype(F32)
    rows = jnp.full((B_HEADS, 3, NA_QR, NA_KR, 2 * NA_WC - 1), -1e30, F32)
    for r in range(2 * NA_WR - 1):
        rows = jnp.where(dr == r, na[:, r][:, None, None, None, :], rows)
    tab = jnp.full((B_HEADS, 3, NA_QR, GRID_W, NA_KR, GRID_W), -1e30, F32)
    for s in range(2 * NA_WC - 1):
        tab = jnp.where(dc == s, rows[:, :, :, None, :, s][..., None], tab)
    return tab.reshape(B_HEADS, 3, NA_QR * GRID_W, NA_KR * GRID_W)


def _na_kernel(rows, q_ref, k_ref, v_ref, tab_ref, o_ref):
    scale = B_HD ** -0.5
    nq = NA_QR * GRID_W
    nk = NA_KR * GRID_W

    def body(j, carry):
        r0 = j * NA_QR
        kr0 = jnp.clip(r0 - NA_WR // 2, 0, rows - NA_KR)
        var = lax.shift_right_logical(r0 - kr0, 2)
        qs = pl.multiple_of(r0 * GRID_W, nq)
        ks = pl.multiple_of(kr0 * GRID_W, (NA_WR // 2) * GRID_W)
        s = _dot_nt(q_ref[pl.ds(qs, nq), :], k_ref[pl.ds(ks, nk), :]) * scale + tab_ref[0, var]
        p = jnp.exp(s - jnp.max(s, axis=-1, keepdims=True))
        p = (p / jnp.sum(p, axis=-1, keepdims=True)).astype(BF16)
        o_ref[pl.ds(qs, nq), :] = _dot(p, v_ref[pl.ds(ks, nk), :]).astype(o_ref.dtype)
        return carry

    lax.fori_loop(0, rows // NA_QR, body, 0, unroll=2)


def neighbourhood_attention(proj, row_off, nbatch, seq, tab):
    rows = seq // GRID_W
    assert rows >= NA_KR and rows % NA_QR == 0 and NA_WR == 8 and row_off % seq == 0
    boff = row_off // seq
    base = (4 * A_HEADS * A_QK + A_HEADS * A_V) // LANES
    nb = B_HEADS * B_HD // LANES

    def spec(sec):
        return pl.BlockSpec((seq, B_HD), lambda b, h: (boff + b, base + sec * nb + h))

    return pl.pallas_call(
        functools.partial(_na_kernel, rows),
        grid=(nbatch, B_HEADS),
        in_specs=[spec(0), spec(1), spec(2),
                  pl.BlockSpec((1, 3, NA_QR * GRID_W, NA_KR * GRID_W), lambda b, h: (h, 0, 0, 0))],
        out_specs=pl.BlockSpec((seq, B_HD), lambda b, h: (b, h)),
        out_shape=jax.ShapeDtypeStruct((nbatch * seq, B_HEADS * B_HD), BF16),
        compiler_params=_params("parallel", "parallel"),
        name="neighbourhood_attention",
    )(proj, proj, proj, tab)


def _dft_channel_tables():
    j = np.arange(C_GD)
    ang = 2.0 * np.pi * ((j[:, None] * j[None, :]) % C_GD) / C_GD
    s = C_GD ** -0.5
    return jnp.asarray(np.cos(ang) * s, BF16), jnp.asarray(np.sin(ang) * s, BF16)


def _dft_position_table(seq):
    r = 64
    k = jnp.arange(seq, dtype=I32)
    jh = jnp.arange(seq // r, dtype=I32)
    jl = jnp.arange(r, dtype=I32)
    step = 2.0 * np.pi / seq
    a = ((jh[:, None] * r * k[None, :]) % seq).astype(F32) * step
    b = ((jl[:, None] * k[None, :]) % seq).astype(F32) * step
    ca, sa = jnp.cos(a)[:, None, :], jnp.sin(a)[:, None, :]
    cb, sb = jnp.cos(b)[None, :, :], jnp.sin(b)[None, :, :]
    scale = seq ** -0.5
    c = ((ca * cb - sa * sb) * scale).reshape(seq, seq)
    s = ((sa * cb + ca * sb) * -scale).reshape(seq, seq)
    return jnp.concatenate([c, s], axis=1).astype(BF16)


def _fourier_channel_kernel(u_ref, cc_ref, sc_ref, o_ref):
    for g in range(C_GROUPS):
        sl = slice(g * C_GD, (g + 1) * C_GD)
        ug = u_ref[:, sl]
        o_ref[0, 0, :, sl] = _dot(ug, cc_ref[...]).astype(o_ref.dtype)
        o_ref[0, 1, :, sl] = _dot(ug, sc_ref[...]).astype(o_ref.dtype)


def _matmul_kernel(a_ref, b_ref, o_ref):
    o_ref[...] = _dot(a_ref[...], b_ref[0]).astype(o_ref.dtype)


def fourier_mix(proj, row_off, nbatch, seq, cc, sc, pos_tab):
    tm = 512
    assert row_off % tm == 0 and seq % tm == 0
    ns = seq // tm
    roff = row_off // tm
    v = pl.pallas_call(
        _fourier_channel_kernel,
        grid=(nbatch, ns),
        in_specs=[pl.BlockSpec((tm, C_WIDTH), lambda b, i: (roff + b * ns + i, 0)),
                  pl.BlockSpec((C_GD, C_GD), lambda b, i: (0, 0)),
                  pl.BlockSpec((C_GD, C_GD), lambda b, i: (0, 0))],
        out_specs=pl.BlockSpec((1, 2, tm, C_WIDTH), lambda b, i: (b, 0, i, 0)),
        out_shape=jax.ShapeDtypeStruct((nbatch, 2, seq, C_WIDTH), BF16),
        compiler_params=_params("parallel", "parallel"),
        name="fourier_channels",
    )(proj, cc, sc)
    v = v.reshape(nbatch, 2 * seq, C_WIDTH)
    tn = 512
    return pl.pallas_call(
        _matmul_kernel,
        grid=(ns, nbatch, C_WIDTH // tn),
        in_specs=[pl.BlockSpec((tm, 2 * seq), lambda i, b, j: (i, 0)),
                  pl.BlockSpec((1, 2 * seq, tn), lambda i, b, j: (b, 0, j))],
        out_specs=pl.BlockSpec((tm, tn), lambda i, b, j: (b * ns + i, j)),
        out_shape=jax.ShapeDtypeStruct((nbatch * seq, C_WIDTH), BF16),
        compiler_params=_params("parallel", "parallel", "parallel"),
        name="fourier_positions",
    )(pos_tab, v)


def _conv_kernel(bg_ref, cg_ref, hv_ref, cw_ref, cb_ref, o_ref):
    u = cg_ref[...].astype(F32) * hv_ref[...].astype(F32)
    seq = u.shape[0]
    row = lax.broadcasted_iota(I32, u.shape, 0)
    prev = jnp.where(row == 0, 0.0, pltpu.roll(u, 1, 0))
    nxt = jnp.where(row == seq - 1, 0.0, pltpu.roll(u, seq - 1, 0))
    y = prev * cw_ref[0:1, :] + u * cw_ref[1:2, :] + nxt * cw_ref[2:3, :] + cb_ref[...]
    o_ref[...] = (bg_ref[...].astype(F32) * y).astype(o_ref.dtype)


def short_conv(proj, row_off, nbatch, seq, cw, cb):
    tc = 256
    assert row_off % seq == 0
    boff = row_off // seq
    nct = D_WIDTH // tc
    base = C_WIDTH // tc

    def spec(sec):
        return pl.BlockSpec((seq, tc), lambda b, j: (boff + b, base + sec * nct + j))

    return pl.pallas_call(
        _conv_kernel,
        grid=(nbatch, nct),
        in_specs=[spec(0), spec(1), spec(2),
                  pl.BlockSpec((3, tc), lambda b, j: (0, j)),
                  pl.BlockSpec((1, tc), lambda b, j: (0, j))],
        out_specs=pl.BlockSpec((seq, tc), lambda b, j: (b, j)),
        out_shape=jax.ShapeDtypeStruct((nbatch * seq, D_WIDTH), BF16),
        compiler_params=_params("parallel", "parallel"),
        name="short_conv",
    )(proj, proj, proj, cw, cb.reshape(1, D_WIDTH))


def _route(logits, bias):
    s = 1.0 / (1.0 + jnp.exp(-logits))
    sel = s + bias
    sv = [s[j * N_GROUPS:(j + 1) * N_GROUPS] for j in range(EPG)]
    cv = [sel[j * N_GROUPS:(j + 1) * N_GROUPS] for j in range(EPG)]
    hi1, lo1 = jnp.maximum(cv[0], cv[1]), jnp.minimum(cv[0], cv[1])
    hi2, lo2 = jnp.maximum(cv[2], cv[3]), jnp.minimum(cv[2], cv[3])
    gscore = jnp.maximum(hi1, hi2) + jnp.maximum(jnp.minimum(hi1, hi2), jnp.maximum(lo1, lo2))
    gidx = lax.broadcasted_iota(I32, gscore.shape, 0).astype(F32)
    gmax = jnp.max(gscore, axis=0, keepdims=True)
    best = jnp.min(jnp.where(gscore == gmax, gidx, float(N_GROUPS)), axis=0, keepdims=True)
    onehot = gidx == best

    def pick(z):
        return jnp.sum(jnp.where(onehot, z, 0.0), axis=0, keepdims=True)

    c = [pick(z) for z in cv]
    w = [pick(z) for z in sv]
    i1, v1 = jnp.zeros_like(best), c[0]
    for j in range(1, EPG):
        better = c[j] > v1
        i1 = jnp.where(better, float(j), i1)
        v1 = jnp.where(better, c[j], v1)
    i2, v2 = jnp.full_like(best, -1.0), jnp.full_like(best, -jnp.inf)
    for j in range(EPG):
        better = (i1 != float(j)) & ((c[j] > v2) | (i2 < 0.0))
        i2 = jnp.where(better, float(j), i2)
        v2 = jnp.where(better, c[j], v2)
    lo = jnp.minimum(i1, i2)
    hi = jnp.maximum(i1, i2)

    def gate(idx):
        g = w[0]
        for j in range(1, EPG):
            g = jnp.where(idx == float(j), w[j], g)
        return g

    w_lo, w_hi = gate(lo), gate(hi)
    total = w_lo + w_hi
    pair = jnp.where(lo == 0.0, 0.0, jnp.where(lo == 1.0, 3.0, 5.0)) + (hi - lo - 1.0)
    cls = (best * float(N_PAIRS) + pair).astype(I32)
    return cls, w_lo / total, w_hi / total


def _outproj_router_kernel(n1, tb_ref, a1_ref, b1_ref, a2_ref, b2_ref, x_ref, g1_ref, sc_ref, sh_ref,
                           g_ref, w_ref, wrh_ref, wrl_ref, br_ref,
                           xo_ref, h_ref, cls_ref, wlo_ref, whi_ref):
    first = pl.program_id(0) < n1
    o1 = jnp.where(first, a1_ref[...], a2_ref[...])
    o2 = jnp.where(first, b1_ref[...], b2_ref[...])
    m = _dot(o1, w_ref[0]) + _dot(o2, w_ref[1])
    x = x_ref[...] + g1_ref[0, 0] * m
    xo_ref[...] = x
    h = _rms(x, g_ref[...], EPS) * (1.0 + sc_ref[0, 0]) + sh_ref[0, 0]
    h_ref[...] = h
    h_hi, h_lo = _split_bf16(h)
    logits = _dot(h_hi, wrh_ref[...]) + _dot(h_lo, wrh_ref[...]) + _dot(h_hi, wrl_ref[...])
    cls, w_lo, w_hi = _route(logits.T[:N_EXPERTS], br_ref[...])
    cls_ref[...] = cls
    wlo_ref[...] = w_lo
    whi_ref[...] = w_hi


def outproj_router(o1, o2, n_first, x, tile_batch, mod4, w_out, g, wr_hi, wr_lo, br):
    t, d = x.shape
    tm = ROUTE_TILE
    assert n_first % tm == 0
    n1 = n_first // tm
    half = w_out.shape[0] // 2

    def first(i, tb):
        return (jnp.minimum(i, n1 - 1), 0)

    def second(i, tb):
        return (jnp.maximum(i - n1, 0), 0)

    def piece(p):
        return pl.BlockSpec((1, 1, 1, d), lambda i, tb: (tb[i], p, 0, 0))

    row = pl.BlockSpec((tm, d), lambda i, tb: (i, 0))
    vec = pl.BlockSpec((1, tm), lambda i, tb: (0, i))
    full = lambda shape: pl.BlockSpec(shape, lambda i, tb: (0,) * len(shape))
    return pl.pallas_call(
        functools.partial(_outproj_router_kernel, n1),
        grid_spec=pltpu.PrefetchScalarGridSpec(
            num_scalar_prefetch=1, grid=(t // tm,),
            in_specs=[pl.BlockSpec((tm, half), first), pl.BlockSpec((tm, half), first),
                      pl.BlockSpec((tm, half), second), pl.BlockSpec((tm, half), second),
                      row, piece(2), piece(4), piece(3), full((1, d)),
                      full((2, half, d)), full((d, LANES)), full((d, LANES)),
                      full((N_EXPERTS, 1))],
            out_specs=[row, row, vec, vec, vec]),
        out_shape=[jax.ShapeDtypeStruct((t, d), F32), jax.ShapeDtypeStruct((t, d), F32),
                   jax.ShapeDtypeStruct((1, t), I32), jax.ShapeDtypeStruct((1, t), F32),
                   jax.ShapeDtypeStruct((1, t), F32)],
        compiler_params=_params("parallel"),
        name="outproj_router",
    )(tile_batch, o1[0], o2[0], o1[1], o2[1], x, mod4, mod4, mod4, g.reshape(1, d),
      w_out.reshape(2, half, d), wr_hi, wr_lo, br)


def _moe_kernel(bm, nsteps, ea_ref, eb_ref, nused_ref,
                src0_ref, src1_ref, src2_ref, dst_ref, h_ref, wa_ref, wb_ref,
                gua_ref, dna_ref, gub_ref, dnb_ref, y_ref, xbuf, ybuf, gsem, ssem):
    i = pl.program_id(0)
    n = nused_ref[0]
    gslot = i % 3
    sslot = i % 2
    ntok = y_ref.shape[0] - 2 * bm

    def gather_row(ids_ref, r, s):
        return pltpu.make_async_copy(h_ref.at[pl.ds(ids_ref[0, 0, r], 1), :],
                                     xbuf.at[s, pl.ds(r, 1), :], gsem.at[s])

    def gather_loop(ids_ref, s):
        def body(c, carry):
            for j in range(DMA_UNROLL):
                gather_row(ids_ref, c * DMA_UNROLL + j, s).start()
            return carry

        lax.fori_loop(0, bm // DMA_UNROLL, body, 0)

    def wait_rows(src, dst, sem):
        def body(c, carry):
            for _ in range(DMA_UNROLL):
                pltpu.make_async_copy(src.at[pl.ds(0, 1), :], dst.at[pl.ds(0, 1), :], sem).wait()
            return carry

        lax.fori_loop(0, bm // DMA_UNROLL, body, 0)

    def wait_gather(s):
        wait_rows(h_ref, xbuf.at[s], gsem.at[s])

    def wait_scatter(s):
        wait_rows(ybuf.at[s], y_ref, ssem.at[s])

    @pl.when(i == 0)
    def _():
        gather_loop(src0_ref, 0)
        gather_loop(src1_ref, 1)
        ybuf[0] = jnp.zeros((bm, ybuf.shape[2]), F32)
        spare = [pltpu.make_async_copy(ybuf.at[0], y_ref.at[pl.ds(ntok + k * bm, bm), :], ssem.at[k])
                 for k in range(2)]
        for c in spare:
            c.start()
        for c in spare:
            c.wait()

    @pl.when(i < n)
    def _():
        wait_gather(gslot)

        @pl.when(i >= 2)
        def _():
            wait_scatter(sslot)

    @pl.when(i < n)
    def _():
        x = xbuf[gslot].astype(BF16)

        def expert(gu_ref, dn_ref):
            gu = _dot(x, gu_ref[0])
            g, u = gu[:, :D_EXPERT], gu[:, D_EXPERT:]
            act = (g / (1.0 + jnp.exp(-g))) * u
            return _dot(act.astype(BF16), dn_ref[0])

        ybuf[sslot] = wa_ref[...] * expert(gua_ref, dna_ref) + wb_ref[...] * expert(gub_ref, dnb_ref)
        nxt = (i + 2) % 3
        for r in range(bm):
            gather_row(src2_ref, r, nxt).start()
        for r in range(bm):
            pltpu.make_async_copy(ybuf.at[sslot, pl.ds(r, 1), :],
                                  y_ref.at[pl.ds(dst_ref[0, 0, r], 1), :], ssem.at[sslot]).start()

    @pl.when(i == nsteps - 1)
    def _():
        wait_gather(n % 3)
        wait_gather((n + 1) % 3)
        wait_scatter((n - 1) % 2)

        @pl.when(n >= 2)
        def _():
            wait_scatter(n % 2)


def moe_experts(h, row_src, row_dst, blk_a, blk_b, n_used, row_wa, row_wb, w_gu, w_dn):
    t, d = h.shape
    bm = MOE_BLOCK
    nrows = row_src.shape[0]
    nblk = nrows // bm
    col = pl.BlockSpec((bm, 1), lambda i, *_: (i, 0))
    ids = lambda f: pl.BlockSpec((1, 1, bm), lambda i, *_: (jnp.minimum(f(i), nblk - 1), 0, 0),
                                 memory_space=pltpu.SMEM)
    gu = lambda sel: pl.BlockSpec((1, d, 2 * D_EXPERT), lambda i, *s: (s[sel][i], 0, 0))
    dn = lambda sel: pl.BlockSpec((1, D_EXPERT, d), lambda i, *s: (s[sel][i], 0, 0))
    hbm = pl.BlockSpec(memory_space=pl.ANY)
    src3 = row_src.reshape(nblk, 1, bm)
    return pl.pallas_call(
        functools.partial(_moe_kernel, bm, nblk),
        grid_spec=pltpu.PrefetchScalarGridSpec(
            num_scalar_prefetch=3, grid=(nblk,),
            in_specs=[ids(lambda i: 0), ids(lambda i: 1), ids(lambda i: i + 2), ids(lambda i: i),
                      hbm, col, col, gu(0), dn(0), gu(1), dn(1)],
            out_specs=hbm,
            scratch_shapes=[pltpu.VMEM((3, bm, d), F32), pltpu.VMEM((2, bm, d), F32),
                            pltpu.SemaphoreType.DMA((3,)), pltpu.SemaphoreType.DMA((2,))]),
        out_shape=jax.ShapeDtypeStruct((t + 2 * bm, d), F32),
        compiler_params=_params("arbitrary"),
        name="moe_experts",
    )(blk_a, blk_b, n_used, src3, src3, src3, row_dst.reshape(nblk, 1, bm), h,
      row_wa.reshape(nrows, 1), row_wb.reshape(nrows, 1), w_gu, w_dn, w_gu, w_dn)


def _dispatch(cls, w_lo, w_hi, expert_base):
    t = cls.shape[0]
    bm = MOE_BLOCK
    nblk = -(-t // bm) + N_CLASSES
    nrows = nblk * bm
    iota = jnp.arange(t, dtype=I32)
    _, order, s_lo, s_hi = lax.sort((cls, iota, w_lo, w_hi), num_keys=1, is_stable=True)
    classes = jnp.arange(N_CLASSES, dtype=I32)
    counts = jnp.sum((cls[:, None] == classes[None, :]).astype(I32), axis=0)
    padded = (counts + bm - 1) // bm * bm
    pend = jnp.cumsum(padded)
    pstart = pend - padded
    start = jnp.cumsum(counts) - counts
    blk_start = jnp.arange(nblk, dtype=I32) * bm
    blk_cls = jnp.minimum(jnp.sum((pend[None, :] <= blk_start[:, None]).astype(I32), axis=-1),
                          N_CLASSES - 1)
    k0 = blk_start - pstart[blk_cls]
    k = k0[:, None] + jnp.arange(bm, dtype=I32)[None, :]
    valid = (k < counts[blk_cls][:, None]).reshape(nrows)
    src = jnp.clip(start[blk_cls][:, None] + k, 0, t - 1).reshape(nrows)
    tok = order[src]
    row_src = jnp.where(valid, tok, 0)
    spare = (t + (jnp.arange(nblk, dtype=I32) % 2)[:, None] * bm
             + jnp.arange(bm, dtype=I32)[None, :]).reshape(nrows)
    row_dst = jnp.where(valid, tok, spare)
    row_wa = jnp.where(valid, s_lo[src], 0.0)
    row_wb = jnp.where(valid, s_hi[src], 0.0)
    grp = blk_cls // N_PAIRS
    pair = blk_cls % N_PAIRS
    blk_a = expert_base + grp * EPG + jnp.asarray(PAIR_LO)[pair]
    blk_b = expert_base + grp * EPG + jnp.asarray(PAIR_HI)[pair]
    n_used = (pend[-1] // bm).reshape(1).astype(I32)
    return row_src, row_dst, row_wa, row_wb, blk_a.astype(I32), blk_b.astype(I32), n_used


def _final_kernel(tb_ref, x_ref, y_ref, g2_ref, g_ref, o_ref):
    x = x_ref[...] + g2_ref[0, 0] * y_ref[...]
    o_ref[...] = _rms(x, g_ref[...], EPS)


def final_norm(x, y, tile_batch, tile_off, nrows, mod4, g):
    d = x.shape[1]
    tm = ROW_TILE
    row = pl.BlockSpec((tm, d), lambda i, tb: (tile_off + i, 0))
    return pl.pallas_call(
        _final_kernel,
        grid_spec=pltpu.PrefetchScalarGridSpec(
            num_scalar_prefetch=1, grid=(nrows // tm,),
            in_specs=[row, row,
                      pl.BlockSpec((1, 1, 1, d), lambda i, tb: (tb[tile_off + i], 5, 0, 0)),
                      pl.BlockSpec((1, d), lambda i, tb: (0, 0))],
            out_specs=pl.BlockSpec((tm, d), lambda i, tb: (i, 0))),
        out_shape=jax.ShapeDtypeStruct((nrows, d), F32),
        compiler_params=_params("parallel"),
        name="final_norm",
    )(tile_batch, x, y, mod4, g.reshape(1, d))


def _tile_batch(tile, groups):
    out = []
    base = 0
    for nb, seq in groups:
        assert seq % tile == 0
        out.append(base + np.repeat(np.arange(nb), seq // tile))
        base += nb
    return jnp.asarray(np.concatenate(out), I32)


def kernel(x_prompt, x_sample, c_prompt, c_sample, w_ada, b_ada, norm_mix_g, norm_ffn_g, w_in_ab, w_out_ab, lambda_q1, lambda_k1, lambda_q2, lambda_k2, subln_g, na_bias, w_in_cd, conv_w, conv_b, w_out_cd, rel_bias, w_router, b_router, w_gate_up, w_down, final_norm_g):
    depth, d = norm_mix_g.shape
    groups = [(x_prompt.shape[0], x_prompt.shape[1]), (x_sample.shape[0], x_sample.shape[1])]
    rows = [nb * seq for nb, seq in groups]
    offs = [0, rows[0]]
    t = rows[0] + rows[1]
    nb_all = groups[0][0] + groups[1][0]
    assert nb_all <= NB_PAD

    x = (x_prompt.reshape(rows[0], d), x_sample.reshape(rows[1], d))
    c_all = jnp.concatenate([c_prompt, c_sample, jnp.zeros((NB_PAD - nb_all, d), F32)], axis=0)
    mod = ada_mod(c_all, w_ada, b_ada).reshape(depth, NB_PAD, 6, 1, d)
    tb_row = _tile_batch(ROW_TILE, groups)
    tb_route = _tile_batch(ROUTE_TILE, groups)
    tb_proj = _tile_batch(PROJ_TILE, groups)

    wr = w_router.astype(F32).reshape(d, N_GROUPS, EPG).transpose(0, 2, 1).reshape(d, N_EXPERTS)
    wr_hi, wr_lo = _split_bf16(jnp.pad(wr, ((0, 0), (0, LANES - N_EXPERTS))))
    br = b_router.astype(F32).reshape(N_GROUPS, EPG).T.reshape(N_EXPERTS, 1)

    cc, sc = _dft_channel_tables()
    w_gu = w_gate_up.astype(BF16).reshape((depth * N_EXPERTS,) + w_gate_up.shape[2:])
    w_dn = w_down.astype(BF16).reshape((depth * N_EXPERTS,) + w_down.shape[2:])
    y = None
    for l in range(depth):
        i = l // 2
        mod4 = mod[l]
        if l % 2 == 0:
            w_in, w_out = w_in_ab[i], w_out_ab[i]
        else:
            w_in, w_out = w_in_cd[i], w_out_cd[i]
        proj, x = norm_matmul(x, tb_proj, mod4, 1, 0, norm_mix_g[l], w_in.astype(BF16),
                              y=y, g2_mod=mod[l - 1] if l else None, g2_piece=5)
        if l % 2 == 0:
            lam_init = 0.8 - 0.6 * math.exp(-0.3 * l)
            lam = (jnp.exp(jnp.sum(lambda_q1[i].astype(F32) * lambda_k1[i].astype(F32)))
                   - jnp.exp(jnp.sum(lambda_q2[i].astype(F32) * lambda_k2[i].astype(F32)))
                   + lam_init).reshape(1)
            rb = rel_bias.astype(F32)
            far = jnp.stack([rb[T5_BUCKETS // 2 - 1], rb[T5_BUCKETS - 1]], axis=1)
            na_tab = _na_tables(na_bias[i])
            subg = subln_g[i].astype(F32).reshape(A_V, 1)
            o1, o2, bands = [], [], {}
            for (nb, seq), off in zip(groups, offs):
                tq = min(ATT_TQ, seq // 4)
                if tq not in bands:
                    bands[tq] = _t5_band_tables(rel_bias, tq)
                o1.append(diff_attention(proj, off, nb, seq, lam, far, bands[tq], subg, lam_init, tq))
                o2.append(neighbourhood_attention(proj, off, nb, seq, na_tab))
        else:
            o1, o2 = [], []
            for (nb, seq), off in zip(groups, offs):
                o1.append(fourier_mix(proj, off, nb, seq, cc, sc, _dft_position_table(seq)))
                o2.append(short_conv(proj, off, nb, seq, conv_w[i].astype(F32), conv_b[i].astype(F32)))
        x, h2, cls, w_lo, w_hi = outproj_router(o1, o2, rows[0], x, tb_route, mod4,
                                                w_out.astype(BF16), norm_ffn_g[l], wr_hi, wr_lo, br)
        row_src, row_dst, row_wa, row_wb, blk_a, blk_b, n_used = _dispatch(
            cls[0], w_lo[0], w_hi[0], l * N_EXPERTS)
        y = moe_experts(h2, row_src, row_dst, blk_a, blk_b, n_used, row_wa, row_wb, w_gu, w_dn)

    mod4 = mod[depth - 1]
    outs = []
    for (nb, seq), off, n in zip(groups, offs, rows):
        o = final_norm(x, y, tb_row, off // ROW_TILE, n, mod4, final_norm_g)
        outs.append(o.reshape(nb, seq, d))
    return tuple(outs)
```

```python
import functools
import math

import numpy as np
import jax
import jax.numpy as jnp
from jax import lax
from jax.experimental import pallas as pl
from jax.experimental.pallas import tpu as pltpu

F32 = jnp.float32
BF16 = jnp.bfloat16
I32 = jnp.int32

GRID_W = 64
A_HEADS = 8
A_QK = 64
A_V = 128
SUBLN_EPS = 1e-5
B_HEADS = 8
B_HD = 128
NA_WR = 8
NA_WC = 16
NA_QR = 8
NA_KR = 16
C_GROUPS = 4
C_GD = 256
C_WIDTH = 1024
D_WIDTH = 1024
T5_BUCKETS = 32
T5_MAX_DIST = 128
N_EXPERTS = 32
N_GROUPS = 8
EPG = 4
D_EXPERT = 512
EPS = 1e-6
N_PAIRS = 6
N_CLASSES = N_GROUPS * N_PAIRS
PAIR_LO = np.array([0, 0, 0, 1, 1, 2], np.int32)
PAIR_HI = np.array([1, 2, 3, 2, 3, 3], np.int32)

LANES = 128
VMEM_LIMIT = 56 * 1024 * 1024

NB_PAD = 16
ROW_TILE = 512
PROJ_TILE = 256
ROUTE_TILE = 256
MOE_BLOCK = 256
ATT_TQ = 256
DMA_UNROLL = 8
SUM_ROWS = 16


def _dot(a, b):
    return jnp.dot(a, b, preferred_element_type=F32)


def _dot_nt(a, b):
    return lax.dot_general(a, b, (((1,), (1,)), ((), ())), preferred_element_type=F32)


def _split_bf16(x):
    hi = x.astype(BF16)
    lo = (x - hi.astype(F32)).astype(BF16)
    return hi, lo


def _params(*sem):
    return pltpu.CompilerParams(dimension_semantics=sem, vmem_limit_bytes=VMEM_LIMIT)


def _ada_kernel(c_ref, w_ref, b_ref, o_ref):
    c = c_ref[...]
    cs = c / (1.0 + jnp.exp(-c))
    cs_hi, cs_lo = _split_bf16(cs)
    w_hi, w_lo = _split_bf16(w_ref[0])
    acc = _dot(cs_hi, w_hi) + _dot(cs_lo, w_hi) + _dot(cs_hi, w_lo)
    o_ref[0] = acc + b_ref[0]


def ada_mod(c_all, w_ada, b_ada):
    depth, d, n = w_ada.shape
    tn = 1024
    return pl.pallas_call(
        _ada_kernel,
        grid=(depth, n // tn),
        in_specs=[
            pl.BlockSpec((NB_PAD, d), lambda l, j: (0, 0)),
            pl.BlockSpec((1, d, tn), lambda l, j: (l, 0, j)),
            pl.BlockSpec((1, 1, tn), lambda l, j: (l, 0, j)),
        ],
        out_specs=pl.BlockSpec((1, NB_PAD, tn), lambda l, j: (l, 0, j)),
        out_shape=jax.ShapeDtypeStruct((depth, NB_PAD, n), F32),
        compiler_params=_params("parallel", "parallel"),
        name="ada_mod",
    )(c_all, w_ada, b_ada.reshape(depth, 1, n))


def _rms(x, g, eps):
    return x * lax.rsqrt(jnp.mean(x * x, axis=-1, keepdims=True) + eps) * g


def _norm_matmul_kernel(n_first, tn, tb_ref, *refs):
    if n_first is None:
        x_ref, y_ref, g2_ref, sc_ref, sh_ref, g_ref, w_ref, o_ref, x2_ref, h_scr = refs
        x = x_ref[...] + g2_ref[0, 0] * y_ref[...]
    else:
        xa_ref, xb_ref, sc_ref, sh_ref, g_ref, w_ref, o_ref, x2_ref, h_scr = refs
        x = jnp.where(pl.program_id(0) < n_first, xa_ref[...], xb_ref[...])
    x2_ref[...] = x
    h = _rms(x, g_ref[...], EPS) * (1.0 + sc_ref[0, 0]) + sh_ref[0, 0]
    h_scr[...] = h.astype(BF16)
    for j in range(w_ref.shape[1] // tn):
        cols = slice(j * tn, (j + 1) * tn)
        o_ref[:, cols] = _dot(h_scr[...], w_ref[:, cols]).astype(o_ref.dtype)


def norm_matmul(x, tile_batch, mod4, sc_piece, sh_piece, g, w, y=None, g2_mod=None, g2_piece=None):
    n = w.shape[1]
    tm, tn = PROJ_TILE, 1024

    def piece(p):
        return pl.BlockSpec((1, 1, 1, d), lambda i, tb: (tb[i], p, 0, 0))

    if y is None:
        xa, xb = x
        d = xa.shape[1]
        t = xa.shape[0] + xb.shape[0]
        assert xa.shape[0] % tm == 0
        n_first = xa.shape[0] // tm
        in_specs = [pl.BlockSpec((tm, d), lambda i, tb: (jnp.minimum(i, n_first - 1), 0)),
                    pl.BlockSpec((tm, d), lambda i, tb: (jnp.maximum(i - n_first, 0), 0))]
        args = [xa, xb]
    else:
        t, d = x.shape
        n_first = None
        in_specs = [pl.BlockSpec((tm, d), lambda i, tb: (i, 0)),
                    pl.BlockSpec((tm, d), lambda i, tb: (i, 0)), piece(g2_piece)]
        args = [x, y, g2_mod]
    in_specs += [piece(sc_piece), piece(sh_piece),
                 pl.BlockSpec((1, d), lambda i, tb: (0, 0)),
                 pl.BlockSpec((d, n), lambda i, tb: (0, 0), pipeline_mode=pl.Buffered(1))]
    args += [mod4, mod4, g.reshape(1, d), w]
    return pl.pallas_call(
        functools.partial(_norm_matmul_kernel, n_first, tn),
        grid_spec=pltpu.PrefetchScalarGridSpec(
            num_scalar_prefetch=1, grid=(t // tm,),
            in_specs=in_specs,
            out_specs=[pl.BlockSpec((tm, n), lambda i, tb: (i, 0)),
                       pl.BlockSpec((tm, d), lambda i, tb: (i, 0))],
            scratch_shapes=[pltpu.VMEM((tm, d), BF16)]),
        out_shape=[jax.ShapeDtypeStruct((t, n), BF16), jax.ShapeDtypeStruct((t, d), F32)],
        compiler_params=_params("parallel"),
        name="norm_matmul",
    )(tile_batch, *args)


def _t5_bucket_np(rel):
    half = T5_BUCKETS // 2
    max_exact = half // 2
    sign = (rel > 0).astype(np.int64) * half
    n = np.abs(rel)
    nf = np.maximum(n, 1).astype(np.float64)
    large = max_exact + (np.log(nf / max_exact) / math.log(T5_MAX_DIST / max_exact)
                         * (half - max_exact)).astype(np.int64)
    large = np.minimum(large, half - 1)
    return (sign + np.where(n < max_exact, n, large)).astype(np.int32)


def _t5_band_tables(rel_bias, tq):
    d = np.arange(-1, 2)[:, None, None]
    kk = np.arange(tq)[None, :, None]
    qq = np.arange(tq)[None, None, :]
    bucket = jnp.asarray(_t5_bucket_np(d * tq + kk - qq)[None].astype(np.int8))
    rb = rel_bias.astype(F32).T[:, :, None, None, None]
    tab = jnp.zeros((A_HEADS, 3, tq, tq), F32)
    for b in range(T5_BUCKETS):
        tab = jnp.where(bucket == b, rb[:, b], tab)
    return tab


def _diffattn_kernel(seq, tq, lam_init, lam_ref, far_ref, q1_ref, q2_ref, k1_ref, k2_ref, v_ref,
                     band_ref, subg_ref, o_ref, vt_scr, s1_scr, s2_scr, p1_scr, p2_scr):
    h = pl.program_id(1)
    qi = pl.program_id(2)
    nch = seq // tq
    grp = tq // 8
    ngrp, _, gkeys = vt_scr.shape

    @pl.when(qi == 0)
    def _():
        for g in range(ngrp):
            vt_scr[g, :A_V, :] = v_ref[g * gkeys:(g + 1) * gkeys, :].astype(F32).T.astype(BF16)
            vt_scr[g, A_V:, :] = jnp.ones((SUM_ROWS, gkeys), BF16)

    lane = lax.broadcasted_iota(I32, (1, LANES), 1)
    head_mask = jnp.where((lane // A_QK) == (h % 2), A_QK ** -0.5, 0.0).astype(BF16)
    q1 = q1_ref[...] * head_mask
    q2 = q2_ref[...] * head_mask
    far_l, far_r = far_ref[h, 0], far_ref[h, 1]

    def fold(x):
        return x.reshape(grp, 8, tq)

    def scores(c, carry):
        m1, m2 = carry
        ks = pl.multiple_of(c * tq, tq)
        d = c - qi
        near = band_ref[0, jnp.clip(d + 1, 0, 2)]
        bias = jnp.where(jnp.abs(d) <= 1, near, jnp.where(d < 0, far_l, far_r))
        s1 = _dot_nt(k1_ref[pl.ds(ks, tq), :], q1) + bias
        s2 = _dot_nt(k2_ref[pl.ds(ks, tq), :], q2) + bias
        s1_scr[pl.ds(ks, tq), :] = s1
        s2_scr[pl.ds(ks, tq), :] = s2
        return (jnp.maximum(m1, jnp.max(fold(s1), axis=0)),
                jnp.maximum(m2, jnp.max(fold(s2), axis=0)))

    neg = jnp.full((8, tq), -jnp.inf, F32)
    m1, m2 = lax.fori_loop(0, nch, scores, (neg, neg), unroll=min(nch, 8))
    m1 = jnp.max(m1, axis=0, keepdims=True)
    m2 = jnp.max(m2, axis=0, keepdims=True)

    acc1 = jnp.zeros((A_V + SUM_ROWS, tq), F32)
    acc2 = jnp.zeros((A_V + SUM_ROWS, tq), F32)
    for g in range(ngrp):
        rows = slice(g * gkeys, (g + 1) * gkeys)
        p1_scr[rows, :] = jnp.exp((s1_scr[rows, :] - m1).astype(BF16))
        p2_scr[rows, :] = jnp.exp((s2_scr[rows, :] - m2).astype(BF16))
        acc1 = acc1 + _dot(vt_scr[g], p1_scr[rows, :])
        acc2 = acc2 + _dot(vt_scr[g], p2_scr[rows, :])
    n1 = 1.0 / acc1[A_V:A_V + 1, :]
    n2 = lam_ref[0] / acc2[A_V:A_V + 1, :]
    ot = acc1[:A_V, :] * n1 - acc2[:A_V, :] * n2
    ms = jnp.mean(ot * ot, axis=0, keepdims=True)
    ot = ot * lax.rsqrt(ms + SUBLN_EPS) * (subg_ref[...] * (1.0 - lam_init))
    o_ref[...] = ot.T.astype(o_ref.dtype)


def diff_attention(proj, row_off, nbatch, seq, lam, far, band, subg_col, lam_init, tq):
    nq = seq // tq
    gch = 4
    assert tq >= LANES and nq >= 3 and nq % gch == 0 and row_off % seq == 0
    boff = row_off // seq
    qoff = row_off // tq
    na = A_HEADS * A_QK // LANES

    def qspec(sec):
        return pl.BlockSpec((tq, LANES), lambda b, h, i: (qoff + b * nq + i, sec * na + h // 2))

    def kspec(sec):
        return pl.BlockSpec((seq, LANES), lambda b, h, i: (boff + b, sec * na + h // 2))

    smem = pl.BlockSpec(memory_space=pltpu.SMEM)
    return pl.pallas_call(
        functools.partial(_diffattn_kernel, seq, tq, lam_init),
        grid=(nbatch, A_HEADS, nq),
        in_specs=[smem, smem, qspec(0), qspec(1), kspec(2), kspec(3),
                  pl.BlockSpec((seq, A_V), lambda b, h, i: (boff + b, 4 * na + h)),
                  pl.BlockSpec((1, 3, tq, tq), lambda b, h, i: (h, 0, 0, 0)),
                  pl.BlockSpec((A_V, 1), lambda b, h, i: (0, 0))],
        out_specs=pl.BlockSpec((tq, A_V), lambda b, h, i: (b * nq + i, h)),
        out_shape=jax.ShapeDtypeStruct((nbatch * seq, A_HEADS * A_V), BF16),
        scratch_shapes=[pltpu.VMEM((nq // gch, A_V + SUM_ROWS, gch * tq), BF16),
                        pltpu.VMEM((seq, tq), F32), pltpu.VMEM((seq, tq), F32),
                        pltpu.VMEM((seq, tq), BF16), pltpu.VMEM((seq, tq), BF16)],
        compiler_params=_params("parallel", "parallel", "arbitrary"),
        name="diff_attention",
    )(lam, far, proj, proj, proj, proj, proj, band, subg_col)


def _na_tables(na_tab):
    half = NA_WR // 2
    qr = np.arange(NA_QR)[:, None]
    kr = np.arange(NA_KR)[None, :]
    dr = np.full((3, NA_QR, NA_KR), -1, np.int64)
    for var, (rs, off) in enumerate([(np.maximum(qr - half, 0), NA_WR - 1),
                                     (qr, half - 1),
                                     (half + np.minimum(qr, half), -1)]):
        dr[var] = np.where((kr >= rs) & (kr < rs + NA_WR), kr - qr + off, -1)
    c = np.arange(GRID_W)[:, None]
    kc = np.arange(GRID_W)[None, :]
    cs = np.clip(c - NA_WC // 2, 0, GRID_W - NA_WC)
    col_ok = (kc >= cs) & (kc < cs + NA_WC)
    dc = jnp.asarray(np.where(col_ok, kc - c + NA_WC - 1, -1).astype(np.int8))[None, None]
    na = na_tab.astype(F32)
    cols = jnp.zeros((B_HEADS, 2 * NA_WR - 1, GRID_W, GRID_W), F32)
    for s in range(2 * NA_WC - 1):
        cols = jnp.where(dc == s, na[:, :, s][:, :, None, None], cols)
    onehot = (dr[..., None] == np.arange(2 * NA_WR - 1)).astype(np.float32)
    tab = jnp.einsum('vqkr,hrcl->hvqckl', jnp.asarray(onehot), cols, precision=lax.Precision.HIGHEST)
    ok = jnp.asarray(dr >= 0)[None, :, :, None, :, None] & jnp.asarray(col_ok)[None, None, None, :, None, :]
    tab = jnp.where(ok, tab, -1e30)
    return tab.reshape(B_HEADS, 3, NA_QR * GRID_W, NA_KR * GRID_W)


def _na_kernel(rows, q_ref, k_ref, v_ref, tab_ref, o_ref):
    scale = B_HD ** -0.5
    nq = NA_QR * GRID_W
    nk = NA_KR * GRID_W

    def body(j, carry):
        r0 = j * NA_QR
        kr0 = jnp.clip(r0 - NA_WR // 2, 0, rows - NA_KR)
        var = lax.shift_right_logical(r0 - kr0, 2)
        qs = pl.multiple_of(r0 * GRID_W, nq)
        ks = pl.multiple_of(kr0 * GRID_W, (NA_WR // 2) * GRID_W)
        s = _dot_nt(q_ref[pl.ds(qs, nq), :], k_ref[pl.ds(ks, nk), :]) * scale + tab_ref[0, var]
        p = jnp.exp(s - jnp.max(s, axis=-1, keepdims=True))
        p = (p / jnp.sum(p, axis=-1, keepdims=True)).astype(BF16)
        o_ref[pl.ds(qs, nq), :] = _dot(p, v_ref[pl.ds(ks, nk), :]).astype(o_ref.dtype)
        return carry

    lax.fori_loop(0, rows // NA_QR, body, 0, unroll=2)


def neighbourhood_attention(proj, row_off, nbatch, seq, tab):
    rows = seq // GRID_W
    assert rows >= NA_KR and rows % NA_QR == 0 and NA_WR == 8 and row_off % seq == 0
    boff = row_off // seq
    base = (4 * A_HEADS * A_QK + A_HEADS * A_V) // LANES
    nb = B_HEADS * B_HD // LANES

    def spec(sec):
        return pl.BlockSpec((seq, B_HD), lambda b, h: (boff + b, base + sec * nb + h))

    return pl.pallas_call(
        functools.partial(_na_kernel, rows),
        grid=(nbatch, B_HEADS),
        in_specs=[spec(0), spec(1), spec(2),
                  pl.BlockSpec((1, 3, NA_QR * GRID_W, NA_KR * GRID_W), lambda b, h: (h, 0, 0, 0))],
        out_specs=pl.BlockSpec((seq, B_HD), lambda b, h: (b, h)),
        out_shape=jax.ShapeDtypeStruct((nbatch * seq, B_HEADS * B_HD), BF16),
        compiler_params=_params("parallel", "parallel"),
        name="neighbourhood_attention",
    )(proj, proj, proj, tab)


def _dft_channel_tables():
    j = np.arange(C_GD)
    ang = 2.0 * np.pi * ((j[:, None] * j[None, :]) % C_GD) / C_GD
    s = C_GD ** -0.5
    return jnp.asarray(np.cos(ang) * s, BF16), jnp.asarray(np.sin(ang) * s, BF16)


def _dft_position_table(seq):
    r = 64
    k = jnp.arange(seq, dtype=I32)
    jh = jnp.arange(seq // r, dtype=I32)
    jl = jnp.arange(r, dtype=I32)
    step = 2.0 * np.pi / seq
    a = ((jh[:, None] * r * k[None, :]) % seq).astype(F32) * step
    b = ((jl[:, None] * k[None, :]) % seq).astype(F32) * step
    ca, sa = jnp.cos(a)[:, None, :], jnp.sin(a)[:, None, :]
    cb, sb = jnp.cos(b)[None, :, :], jnp.sin(b)[None, :, :]
    scale = seq ** -0.5
    c = ((ca * cb - sa * sb) * scale).reshape(seq, seq)
    s = ((sa * cb + ca * sb) * -scale).reshape(seq, seq)
    return jnp.concatenate([c, s], axis=1).astype(BF16)


def _fourier_channel_kernel(u_ref, cc_ref, sc_ref, o_ref):
    for g in range(C_GROUPS):
        sl = slice(g * C_GD, (g + 1) * C_GD)
        ug = u_ref[:, sl]
        o_ref[0, 0, :, sl] = _dot(ug, cc_ref[...]).astype(o_ref.dtype)
        o_ref[0, 1, :, sl] = _dot(ug, sc_ref[...]).astype(o_ref.dtype)


def _matmul_kernel(a_ref, b_ref, o_ref):
    o_ref[...] = _dot(a_ref[...], b_ref[0]).astype(o_ref.dtype)


def fourier_mix(proj, row_off, nbatch, seq, cc, sc, pos_tab):
    tm = 512
    assert row_off % tm == 0 and seq % tm == 0
    ns = seq // tm
    roff = row_off // tm
    v = pl.pallas_call(
        _fourier_channel_kernel,
        grid=(nbatch, ns),
        in_specs=[pl.BlockSpec((tm, C_WIDTH), lambda b, i: (roff + b * ns + i, 0)),
                  pl.BlockSpec((C_GD, C_GD), lambda b, i: (0, 0)),
                  pl.BlockSpec((C_GD, C_GD), lambda b, i: (0, 0))],
        out_specs=pl.BlockSpec((1, 2, tm, C_WIDTH), lambda b, i: (b, 0, i, 0)),
        out_shape=jax.ShapeDtypeStruct((nbatch, 2, seq, C_WIDTH), BF16),
        compiler_params=_params("parallel", "parallel"),
        name="fourier_channels",
    )(proj, cc, sc)
    v = v.reshape(nbatch, 2 * seq, C_WIDTH)
    tn = 512
    return pl.pallas_call(
        _matmul_kernel,
        grid=(ns, nbatch, C_WIDTH // tn),
        in_specs=[pl.BlockSpec((tm, 2 * seq), lambda i, b, j: (i, 0)),
                  pl.BlockSpec((1, 2 * seq, tn), lambda i, b, j: (b, 0, j))],
        out_specs=pl.BlockSpec((tm, tn), lambda i, b, j: (b * ns + i, j)),
        out_shape=jax.ShapeDtypeStruct((nbatch * seq, C_WIDTH), BF16),
        compiler_params=_params("parallel", "parallel", "parallel"),
        name="fourier_positions",
    )(pos_tab, v)


def _conv_kernel(bg_ref, cg_ref, hv_ref, cw_ref, cb_ref, o_ref):
    u = cg_ref[...].astype(F32) * hv_ref[...].astype(F32)
    seq = u.shape[0]
    row = lax.broadcasted_iota(I32, u.shape, 0)
    prev = jnp.where(row == 0, 0.0, pltpu.roll(u, 1, 0))
    nxt = jnp.where(row == seq - 1, 0.0, pltpu.roll(u, seq - 1, 0))
    y = prev * cw_ref[0:1, :] + u * cw_ref[1:2, :] + nxt * cw_ref[2:3, :] + cb_ref[...]
    o_ref[...] = (bg_ref[...].astype(F32) * y).astype(o_ref.dtype)


def short_conv(proj, row_off, nbatch, seq, cw, cb):
    tc = 256
    assert row_off % seq == 0
    boff = row_off // seq
    nct = D_WIDTH // tc
    base = C_WIDTH // tc

    def spec(sec):
        return pl.BlockSpec((seq, tc), lambda b, j: (boff + b, base + sec * nct + j))

    return pl.pallas_call(
        _conv_kernel,
        grid=(nbatch, nct),
        in_specs=[spec(0), spec(1), spec(2),
                  pl.BlockSpec((3, tc), lambda b, j: (0, j)),
                  pl.BlockSpec((1, tc), lambda b, j: (0, j))],
        out_specs=pl.BlockSpec((seq, tc), lambda b, j: (b, j)),
        out_shape=jax.ShapeDtypeStruct((nbatch * seq, D_WIDTH), BF16),
        compiler_params=_params("parallel", "parallel"),
        name="short_conv",
    )(proj, proj, proj, cw, cb.reshape(1, D_WIDTH))


def _route(logits, bias):
    s = 1.0 / (1.0 + jnp.exp(-logits))
    sel = s + bias
    sv = [s[j * N_GROUPS:(j + 1) * N_GROUPS] for j in range(EPG)]
    cv = [sel[j * N_GROUPS:(j + 1) * N_GROUPS] for j in range(EPG)]
    hi1, lo1 = jnp.maximum(cv[0], cv[1]), jnp.minimum(cv[0], cv[1])
    hi2, lo2 = jnp.maximum(cv[2], cv[3]), jnp.minimum(cv[2], cv[3])
    gscore = jnp.maximum(hi1, hi2) + jnp.maximum(jnp.minimum(hi1, hi2), jnp.maximum(lo1, lo2))
    gidx = lax.broadcasted_iota(I32, gscore.shape, 0).astype(F32)
    gmax = jnp.max(gscore, axis=0, keepdims=True)
    best = jnp.min(jnp.where(gscore == gmax, gidx, float(N_GROUPS)), axis=0, keepdims=True)
    onehot = gidx == best

    def pick(z):
        return jnp.sum(jnp.where(onehot, z, 0.0), axis=0, keepdims=True)

    c = [pick(z) for z in cv]
    w = [pick(z) for z in sv]
    i1, v1 = jnp.zeros_like(best), c[0]
    for j in range(1, EPG):
        better = c[j] > v1
        i1 = jnp.where(better, float(j), i1)
        v1 = jnp.where(better, c[j], v1)
    i2, v2 = jnp.full_like(best, -1.0), jnp.full_like(best, -jnp.inf)
    for j in range(EPG):
        better = (i1 != float(j)) & ((c[j] > v2) | (i2 < 0.0))
        i2 = jnp.where(better, float(j), i2)
        v2 = jnp.where(better, c[j], v2)
    lo = jnp.minimum(i1, i2)
    hi = jnp.maximum(i1, i2)

    def gate(idx):
        g = w[0]
        for j in range(1, EPG):
            g = jnp.where(idx == float(j), w[j], g)
        return g

    w_lo, w_hi = gate(lo), gate(hi)
    total = w_lo + w_hi
    pair = jnp.where(lo == 0.0, 0.0, jnp.where(lo == 1.0, 3.0, 5.0)) + (hi - lo - 1.0)
    cls = (best * float(N_PAIRS) + pair).astype(I32)
    return cls, w_lo / total, w_hi / total


def _outproj_router_kernel(n1, tb_ref, a1_ref, b1_ref, a2_ref, b2_ref, x_ref, g1_ref, sc_ref, sh_ref,
                           g_ref, w_ref, wrh_ref, wrl_ref, br_ref,
                           xo_ref, h_ref, cls_ref, wlo_ref, whi_ref):
    first = pl.program_id(0) < n1
    o1 = jnp.where(first, a1_ref[...], a2_ref[...])
    o2 = jnp.where(first, b1_ref[...], b2_ref[...])
    m = _dot(o1, w_ref[0]) + _dot(o2, w_ref[1])
    x = x_ref[...] + g1_ref[0, 0] * m
    xo_ref[...] = x
    h = _rms(x, g_ref[...], EPS) * (1.0 + sc_ref[0, 0]) + sh_ref[0, 0]
    h_ref[...] = h
    h_hi, h_lo = _split_bf16(h)
    logits = _dot(h_hi, wrh_ref[...]) + _dot(h_lo, wrh_ref[...]) + _dot(h_hi, wrl_ref[...])
    cls, w_lo, w_hi = _route(logits.T[:N_EXPERTS], br_ref[...])
    cls_ref[...] = cls
    wlo_ref[...] = w_lo
    whi_ref[...] = w_hi


def outproj_router(o1, o2, n_first, x, tile_batch, mod4, w_out, g, wr_hi, wr_lo, br):
    t, d = x.shape
    tm = ROUTE_TILE
    assert n_first % tm == 0
    n1 = n_first // tm
    half = w_out.shape[0] // 2

    def first(i, tb):
        return (jnp.minimum(i, n1 - 1), 0)

    def second(i, tb):
        return (jnp.maximum(i - n1, 0), 0)

    def piece(p):
        return pl.BlockSpec((1, 1, 1, d), lambda i, tb: (tb[i], p, 0, 0))

    row = pl.BlockSpec((tm, d), lambda i, tb: (i, 0))
    vec = pl.BlockSpec((1, tm), lambda i, tb: (0, i))
    full = lambda shape: pl.BlockSpec(shape, lambda i, tb: (0,) * len(shape))
    return pl.pallas_call(
        functools.partial(_outproj_router_kernel, n1),
        grid_spec=pltpu.PrefetchScalarGridSpec(
            num_scalar_prefetch=1, grid=(t // tm,),
            in_specs=[pl.BlockSpec((tm, half), first), pl.BlockSpec((tm, half), first),
                      pl.BlockSpec((tm, half), second), pl.BlockSpec((tm, half), second),
                      row, piece(2), piece(4), piece(3), full((1, d)),
                      full((2, half, d)), full((d, LANES)), full((d, LANES)),
                      full((N_EXPERTS, 1))],
            out_specs=[row, row, vec, vec, vec]),
        out_shape=[jax.ShapeDtypeStruct((t, d), F32), jax.ShapeDtypeStruct((t, d), F32),
                   jax.ShapeDtypeStruct((1, t), I32), jax.ShapeDtypeStruct((1, t), F32),
                   jax.ShapeDtypeStruct((1, t), F32)],
        compiler_params=_params("parallel"),
        name="outproj_router",
    )(tile_batch, o1[0], o2[0], o1[1], o2[1], x, mod4, mod4, mod4, g.reshape(1, d),
      w_out.reshape(2, half, d), wr_hi, wr_lo, br)


def _moe_kernel(bm, nsteps, ea_ref, eb_ref, nused_ref,
                src0_ref, src1_ref, src2_ref, dst_ref, h_ref, wa_ref, wb_ref,
                gua_ref, dna_ref, gub_ref, dnb_ref, y_ref, xbuf, ybuf, gsem, ssem):
    i = pl.program_id(0)
    n = nused_ref[0]
    gslot = i % 3
    sslot = i % 2
    ntok = y_ref.shape[0] - 2 * bm

    def gather_row(ids_ref, r, s):
        return pltpu.make_async_copy(h_ref.at[pl.ds(ids_ref[0, 0, r], 1), :],
                                     xbuf.at[s, pl.ds(r, 1), :], gsem.at[s])

    def gather_loop(ids_ref, s):
        def body(c, carry):
            for j in range(DMA_UNROLL):
                gather_row(ids_ref, c * DMA_UNROLL + j, s).start()
            return carry

        lax.fori_loop(0, bm // DMA_UNROLL, body, 0)

    def wait_rows(src, dst, sem):
        def body(c, carry):
            for _ in range(DMA_UNROLL):
                pltpu.make_async_copy(src.at[pl.ds(0, 1), :], dst.at[pl.ds(0, 1), :], sem).wait()
            return carry

        lax.fori_loop(0, bm // DMA_UNROLL, body, 0)

    def wait_gather(s):
        wait_rows(h_ref, xbuf.at[s], gsem.at[s])

    def wait_scatter(s):
        wait_rows(ybuf.at[s], y_ref, ssem.at[s])

    @pl.when(i == 0)
    def _():
        gather_loop(src0_ref, 0)
        gather_loop(src1_ref, 1)
        ybuf[0] = jnp.zeros((bm, ybuf.shape[2]), F32)
        spare = [pltpu.make_async_copy(ybuf.at[0], y_ref.at[pl.ds(ntok + k * bm, bm), :], ssem.at[k])
                 for k in range(2)]
        for c in spare:
            c.start()
        for c in spare:
            c.wait()

    @pl.when(i < n)
    def _():
        wait_gather(gslot)

        @pl.when(i >= 2)
        def _():
            wait_scatter(sslot)

    @pl.when(i < n)
    def _():
        x = xbuf[gslot].astype(BF16)

        def expert(gu_ref, dn_ref):
            gu = _dot(x, gu_ref[0])
            g, u = gu[:, :D_EXPERT], gu[:, D_EXPERT:]
            act = (g / (1.0 + jnp.exp(-g))) * u
            return _dot(act.astype(BF16), dn_ref[0])

        ybuf[sslot] = wa_ref[...] * expert(gua_ref, dna_ref) + wb_ref[...] * expert(gub_ref, dnb_ref)
        nxt = (i + 2) % 3
        for r in range(bm):
            gather_row(src2_ref, r, nxt).start()
        for r in range(bm):
            pltpu.make_async_copy(ybuf.at[sslot, pl.ds(r, 1), :],
                                  y_ref.at[pl.ds(dst_ref[0, 0, r], 1), :], ssem.at[sslot]).start(priority=1)

    @pl.when(i == nsteps - 1)
    def _():
        wait_gather(n % 3)
        wait_gather((n + 1) % 3)
        wait_scatter((n - 1) % 2)

        @pl.when(n >= 2)
        def _():
            wait_scatter(n % 2)


def moe_experts(h, row_src, row_dst, blk_a, blk_b, n_used, row_wa, row_wb, w_gu, w_dn):
    t, d = h.shape
    bm = MOE_BLOCK
    nrows = row_src.shape[0]
    nblk = nrows // bm
    col = pl.BlockSpec((bm, 1), lambda i, *_: (i, 0))
    ids = lambda f: pl.BlockSpec((1, 1, bm), lambda i, *_: (jnp.minimum(f(i), nblk - 1), 0, 0),
                                 memory_space=pltpu.SMEM)
    gu = lambda sel: pl.BlockSpec((1, d, 2 * D_EXPERT), lambda i, *s: (s[sel][i], 0, 0))
    dn = lambda sel: pl.BlockSpec((1, D_EXPERT, d), lambda i, *s: (s[sel][i], 0, 0))
    hbm = pl.BlockSpec(memory_space=pl.ANY)
    src3 = row_src.reshape(nblk, 1, bm)
    return pl.pallas_call(
        functools.partial(_moe_kernel, bm, nblk),
        grid_spec=pltpu.PrefetchScalarGridSpec(
            num_scalar_prefetch=3, grid=(nblk,),
            in_specs=[ids(lambda i: 0), ids(lambda i: 1), ids(lambda i: i + 2), ids(lambda i: i),
                      hbm, col, col, gu(0), dn(0), gu(1), dn(1)],
            out_specs=hbm,
            scratch_shapes=[pltpu.VMEM((3, bm, d), F32), pltpu.VMEM((2, bm, d), F32),
                            pltpu.SemaphoreType.DMA((3,)), pltpu.SemaphoreType.DMA((2,))]),
        out_shape=jax.ShapeDtypeStruct((t + 2 * bm, d), F32),
        compiler_params=_params("arbitrary"),
        name="moe_experts",
    )(blk_a, blk_b, n_used, src3, src3, src3, row_dst.reshape(nblk, 1, bm), h,
      row_wa.reshape(nrows, 1), row_wb.reshape(nrows, 1), w_gu, w_dn, w_gu, w_dn)


def _dispatch(cls, w_lo, w_hi, expert_base):
    t = cls.shape[0]
    bm = MOE_BLOCK
    nblk = -(-t // bm) + N_CLASSES
    nrows = nblk * bm
    iota = jnp.arange(t, dtype=I32)
    _, order, s_lo, s_hi = lax.sort((cls, iota, w_lo, w_hi), num_keys=1, is_stable=True)
    classes = jnp.arange(N_CLASSES, dtype=I32)
    counts = jnp.sum((cls[:, None] == classes[None, :]).astype(I32), axis=0)
    padded = (counts + bm - 1) // bm * bm
    pend = jnp.cumsum(padded)
    pstart = pend - padded
    start = jnp.cumsum(counts) - counts
    blk_start = jnp.arange(nblk, dtype=I32) * bm
    blk_cls = jnp.minimum(jnp.sum((pend[None, :] <= blk_start[:, None]).astype(I32), axis=-1),
                          N_CLASSES - 1)
    k0 = blk_start - pstart[blk_cls]
    k = k0[:, None] + jnp.arange(bm, dtype=I32)[None, :]
    valid = (k < counts[blk_cls][:, None]).reshape(nrows)
    src = jnp.clip(start[blk_cls][:, None] + k, 0, t - 1).reshape(nrows)
    tok = order[src]
    row_src = jnp.where(valid, tok, 0)
    spare = (t + (jnp.arange(nblk, dtype=I32) % 2)[:, None] * bm
             + jnp.arange(bm, dtype=I32)[None, :]).reshape(nrows)
    row_dst = jnp.where(valid, tok, spare)
    row_wa = jnp.where(valid, s_lo[src], 0.0)
    row_wb = jnp.where(valid, s_hi[src], 0.0)
    grp = blk_cls // N_PAIRS
    pair = blk_cls % N_PAIRS
    blk_a = expert_base + grp * EPG + jnp.asarray(PAIR_LO)[pair]
    blk_b = expert_base + grp * EPG + jnp.asarray(PAIR_HI)[pair]
    n_used = (pend[-1] // bm).reshape(1).astype(I32)
    return row_src, row_dst, row_wa, row_wb, blk_a.astype(I32), blk_b.astype(I32), n_used


def _final_kernel(tb_ref, x_ref, y_ref, g2_ref, g_ref, o_ref):
    x = x_ref[...] + g2_ref[0, 0] * y_ref[...]
    o_ref[...] = _rms(x, g_ref[...], EPS)


def final_norm(x, y, tile_batch, tile_off, nrows, mod4, g):
    d = x.shape[1]
    tm = ROW_TILE
    row = pl.BlockSpec((tm, d), lambda i, tb: (tile_off + i, 0))
    return pl.pallas_call(
        _final_kernel,
        grid_spec=pltpu.PrefetchScalarGridSpec(
            num_scalar_prefetch=1, grid=(nrows // tm,),
            in_specs=[row, row,
                      pl.BlockSpec((1, 1, 1, d), lambda i, tb: (tb[tile_off + i], 5, 0, 0)),
                      pl.BlockSpec((1, d), lambda i, tb: (0, 0))],
            out_specs=pl.BlockSpec((tm, d), lambda i, tb: (i, 0))),
        out_shape=jax.ShapeDtypeStruct((nrows, d), F32),
        compiler_params=_params("parallel"),
        name="final_norm",
    )(tile_batch, x, y, mod4, g.reshape(1, d))


def _tile_batch(tile, groups):
    out = []
    base = 0
    for nb, seq in groups:
        assert seq % tile == 0
        out.append(base + np.repeat(np.arange(nb), seq // tile))
        base += nb
    return jnp.asarray(np.concatenate(out), I32)


def kernel(x_prompt, x_sample, c_prompt, c_sample, w_ada, b_ada, norm_mix_g, norm_ffn_g, w_in_ab, w_out_ab, lambda_q1, lambda_k1, lambda_q2, lambda_k2, subln_g, na_bias, w_in_cd, conv_w, conv_b, w_out_cd, rel_bias, w_router, b_router, w_gate_up, w_down, final_norm_g):
    depth, d = norm_mix_g.shape
    groups = [(x_prompt.shape[0], x_prompt.shape[1]), (x_sample.shape[0], x_sample.shape[1])]
    rows = [nb * seq for nb, seq in groups]
    offs = [0, rows[0]]
    t = rows[0] + rows[1]
    nb_all = groups[0][0] + groups[1][0]
    assert nb_all <= NB_PAD

    x = (x_prompt.reshape(rows[0], d), x_sample.reshape(rows[1], d))
    c_all = jnp.concatenate([c_prompt, c_sample, jnp.zeros((NB_PAD - nb_all, d), F32)], axis=0)
    mod = ada_mod(c_all, w_ada, b_ada).reshape(depth, NB_PAD, 6, 1, d)
    tb_row = _tile_batch(ROW_TILE, groups)
    tb_route = _tile_batch(ROUTE_TILE, groups)
    tb_proj = _tile_batch(PROJ_TILE, groups)

    wr = w_router.astype(F32).reshape(d, N_GROUPS, EPG).transpose(0, 2, 1).reshape(d, N_EXPERTS)
    wr_hi, wr_lo = _split_bf16(jnp.pad(wr, ((0, 0), (0, LANES - N_EXPERTS))))
    br = b_router.astype(F32).reshape(N_GROUPS, EPG).T.reshape(N_EXPERTS, 1)

    cc, sc = _dft_channel_tables()
    w_gu = w_gate_up.astype(BF16).reshape((depth * N_EXPERTS,) + w_gate_up.shape[2:])
    w_dn = w_down.astype(BF16).reshape((depth * N_EXPERTS,) + w_down.shape[2:])
    y = None
    for l in range(depth):
        i = l // 2
        mod4 = mod[l]
        if l % 2 == 0:
            w_in, w_out = w_in_ab[i], w_out_ab[i]
        else:
            w_in, w_out = w_in_cd[i], w_out_cd[i]
        proj, x = norm_matmul(x, tb_proj, mod4, 1, 0, norm_mix_g[l], w_in.astype(BF16),
                              y=y, g2_mod=mod[l - 1] if l else None, g2_piece=5)
        if l % 2 == 0:
            lam_init = 0.8 - 0.6 * math.exp(-0.3 * l)
            lam = (jnp.exp(jnp.sum(lambda_q1[i].astype(F32) * lambda_k1[i].astype(F32)))
                   - jnp.exp(jnp.sum(lambda_q2[i].astype(F32) * lambda_k2[i].astype(F32)))
                   + lam_init).reshape(1)
            rb = rel_bias.astype(F32)
            far = jnp.stack([rb[T5_BUCKETS // 2 - 1], rb[T5_BUCKETS - 1]], axis=1)
            na_tab = _na_tables(na_bias[i])
            subg = subln_g[i].astype(F32).reshape(A_V, 1)
            o1, o2, bands = [], [], {}
            for (nb, seq), off in zip(groups, offs):
                tq = min(ATT_TQ, seq // 4)
                if tq not in bands:
                    bands[tq] = _t5_band_tables(rel_bias, tq)
                o1.append(diff_attention(proj, off, nb, seq, lam, far, bands[tq], subg, lam_init, tq))
                o2.append(neighbourhood_attention(proj, off, nb, seq, na_tab))
        else:
            o1, o2 = [], []
            for (nb, seq), off in zip(groups, offs):
                o1.append(fourier_mix(proj, off, nb, seq, cc, sc, _dft_position_table(seq)))
                o2.append(short_conv(proj, off, nb, seq, conv_w[i].astype(F32), conv_b[i].astype(F32)))
        x, h2, cls, w_lo, w_hi = outproj_router(o1, o2, rows[0], x, tb_route, mod4,
                                                w_out.astype(BF16), norm_ffn_g[l], wr_hi, wr_lo, br)
        row_src, row_dst, row_wa, row_wb, blk_a, blk_b, n_used = _dispatch(
            cls[0], w_lo[0], w_hi[0], l * N_EXPERTS)
        y = moe_experts(h2, row_src, row_dst, blk_a, blk_b, n_used, row_wa, row_wb, w_gu, w_dn)

    mod4 = mod[depth - 1]
    outs = []
    for (nb, seq), off, n in zip(groups, offs, rows):
        o = final_norm(x, y, tb_row, off // ROW_TILE, n, mod4, final_norm_g)
        outs.append(o.reshape(nb, seq, d))
    return tuple(outs)
```

```python
import functools
import math

import numpy as np
import jax
import jax.numpy as jnp
from jax import lax
from jax.experimental import pallas as pl
from jax.experimental.pallas import tpu as pltpu

F32 = jnp.float32
BF16 = jnp.bfloat16
I32 = jnp.int32

GRID_W = 64
A_HEADS = 8
A_QK = 64
A_V = 128
SUBLN_EPS = 1e-5
B_HEADS = 8
B_HD = 128
NA_WR = 8
NA_WC = 16
NA_QR = 8
NA_KR = 16
C_GROUPS = 4
C_GD = 256
C_WIDTH = 1024
D_WIDTH = 1024
T5_BUCKETS = 32
T5_MAX_DIST = 128
N_EXPERTS = 32
N_GROUPS = 8
EPG = 4
D_EXPERT = 512
EPS = 1e-6
N_PAIRS = 6
N_CLASSES = N_GROUPS * N_PAIRS
PAIR_LO = np.array([0, 0, 0, 1, 1, 2], np.int32)
PAIR_HI = np.array([1, 2, 3, 2, 3, 3], np.int32)

LANES = 128
VMEM_LIMIT = 56 * 1024 * 1024

NB_PAD = 16
ROW_TILE = 512
PROJ_TILE = 256
ROUTE_TILE = 256
MOE_BLOCK = 256
ATT_TQ = 256
ATT_TILES = 2
DMA_UNROLL = 8
SUM_ROWS = 16


def _dot(a, b):
    return jnp.dot(a, b, preferred_element_type=F32)


def _dot_nt(a, b):
    return lax.dot_general(a, b, (((1,), (1,)), ((), ())), preferred_element_type=F32)


def _split_bf16(x):
    hi = x.astype(BF16)
    lo = (x - hi.astype(F32)).astype(BF16)
    return hi, lo


def _params(*sem):
    return pltpu.CompilerParams(dimension_semantics=sem, vmem_limit_bytes=VMEM_LIMIT)


def _ada_kernel(c_ref, w_ref, b_ref, o_ref):
    c = c_ref[...]
    cs = c / (1.0 + jnp.exp(-c))
    cs_hi, cs_lo = _split_bf16(cs)
    w_hi, w_lo = _split_bf16(w_ref[0])
    acc = _dot(cs_hi, w_hi) + _dot(cs_lo, w_hi) + _dot(cs_hi, w_lo)
    o_ref[0] = acc + b_ref[0]


def ada_mod(c_all, w_ada, b_ada):
    depth, d, n = w_ada.shape
    tn = 1024
    return pl.pallas_call(
        _ada_kernel,
        grid=(depth, n // tn),
        in_specs=[
            pl.BlockSpec((NB_PAD, d), lambda l, j: (0, 0)),
            pl.BlockSpec((1, d, tn), lambda l, j: (l, 0, j)),
            pl.BlockSpec((1, 1, tn), lambda l, j: (l, 0, j)),
        ],
        out_specs=pl.BlockSpec((1, NB_PAD, tn), lambda l, j: (l, 0, j)),
        out_shape=jax.ShapeDtypeStruct((depth, NB_PAD, n), F32),
        compiler_params=_params("parallel", "parallel"),
        name="ada_mod",
    )(c_all, w_ada, b_ada.reshape(depth, 1, n))


def _rms(x, g, eps):
    return x * lax.rsqrt(jnp.mean(x * x, axis=-1, keepdims=True) + eps) * g


def _norm_matmul_kernel(n_first, tn, tb_ref, *refs):
    if n_first is None:
        x_ref, y_ref, g2_ref, sc_ref, sh_ref, g_ref, w_ref, o_ref, x2_ref, h_scr = refs
        x = x_ref[...] + g2_ref[0, 0] * y_ref[...]
    else:
        xa_ref, xb_ref, sc_ref, sh_ref, g_ref, w_ref, o_ref, x2_ref, h_scr = refs
        x = jnp.where(pl.program_id(0) < n_first, xa_ref[...], xb_ref[...])
    x2_ref[...] = x
    h = _rms(x, g_ref[...], EPS) * (1.0 + sc_ref[0, 0]) + sh_ref[0, 0]
    h_scr[...] = h.astype(BF16)
    for j in range(w_ref.shape[1] // tn):
        cols = slice(j * tn, (j + 1) * tn)
        o_ref[:, cols] = _dot(h_scr[...], w_ref[:, cols]).astype(o_ref.dtype)


def norm_matmul(x, tile_batch, mod4, sc_piece, sh_piece, g, w, y=None, g2_mod=None, g2_piece=None):
    n = w.shape[1]
    tm, tn = PROJ_TILE, 1024

    def piece(p):
        return pl.BlockSpec((1, 1, 1, d), lambda i, tb: (tb[i], p, 0, 0))

    if y is None:
        xa, xb = x
        d = xa.shape[1]
        t = xa.shape[0] + xb.shape[0]
        assert xa.shape[0] % tm == 0
        n_first = xa.shape[0] // tm
        in_specs = [pl.BlockSpec((tm, d), lambda i, tb: (jnp.minimum(i, n_first - 1), 0)),
                    pl.BlockSpec((tm, d), lambda i, tb: (jnp.maximum(i - n_first, 0), 0))]
        args = [xa, xb]
    else:
        t, d = x.shape
        n_first = None
        in_specs = [pl.BlockSpec((tm, d), lambda i, tb: (i, 0)),
                    pl.BlockSpec((tm, d), lambda i, tb: (i, 0)), piece(g2_piece)]
        args = [x, y, g2_mod]
    in_specs += [piece(sc_piece), piece(sh_piece),
                 pl.BlockSpec((1, d), lambda i, tb: (0, 0)),
                 pl.BlockSpec((d, n), lambda i, tb: (0, 0), pipeline_mode=pl.Buffered(1))]
    args += [mod4, mod4, g.reshape(1, d), w]
    return pl.pallas_call(
        functools.partial(_norm_matmul_kernel, n_first, tn),
        grid_spec=pltpu.PrefetchScalarGridSpec(
            num_scalar_prefetch=1, grid=(t // tm,),
            in_specs=in_specs,
            out_specs=[pl.BlockSpec((tm, n), lambda i, tb: (i, 0)),
                       pl.BlockSpec((tm, d), lambda i, tb: (i, 0))],
            scratch_shapes=[pltpu.VMEM((tm, d), BF16)]),
        out_shape=[jax.ShapeDtypeStruct((t, n), BF16), jax.ShapeDtypeStruct((t, d), F32)],
        compiler_params=_params("parallel"),
        name="norm_matmul",
    )(tile_batch, *args)


def _t5_bucket_np(rel):
    half = T5_BUCKETS // 2
    max_exact = half // 2
    sign = (rel > 0).astype(np.int64) * half
    n = np.abs(rel)
    nf = np.maximum(n, 1).astype(np.float64)
    large = max_exact + (np.log(nf / max_exact) / math.log(T5_MAX_DIST / max_exact)
                         * (half - max_exact)).astype(np.int64)
    large = np.minimum(large, half - 1)
    return (sign + np.where(n < max_exact, n, large)).astype(np.int32)


def _t5_band_tables(rel_bias, tq):
    d = np.arange(-1, 2)[:, None, None]
    kk = np.arange(tq)[None, :, None]
    qq = np.arange(tq)[None, None, :]
    bucket = jnp.asarray(_t5_bucket_np(d * tq + kk - qq)[None].astype(np.int8))
    rb = rel_bias.astype(F32).T[:, :, None, None, None]
    tab = jnp.zeros((A_HEADS, 3, tq, tq), F32)
    for b in range(T5_BUCKETS):
        tab = jnp.where(bucket == b, rb[:, b], tab)
    return tab


def _diffattn_kernel(seq, tq, lam_init, lam_ref, far_ref, q1_ref, q2_ref, k1_ref, k2_ref, v_ref,
                     band_ref, subg_ref, o_ref, vt_scr, p1_scr, p2_scr, acc_scr, *s_scr):
    s1_scr, s2_scr = s_scr[0::2], s_scr[1::2]
    h = pl.program_id(1)
    step = pl.program_id(2)
    nch = seq // tq
    grp = tq // 8
    ngrp, _, gkeys = vt_scr.shape
    gch = gkeys // tq

    @pl.when(step == 0)
    def _():
        for g in range(ngrp):
            vt_scr[g, :A_V, :] = v_ref[g * gkeys:(g + 1) * gkeys, :].astype(F32).T.astype(BF16)
            vt_scr[g, A_V:, :] = jnp.ones((SUM_ROWS, gkeys), BF16)

    lane = lax.broadcasted_iota(I32, (1, LANES), 1)
    head_mask = jnp.where((lane // A_QK) == (h % 2), A_QK ** -0.5, 0.0).astype(BF16)
    far_l, far_r = far_ref[h, 0], far_ref[h, 1]
    neg = jnp.full((8, tq), -jnp.inf, F32)

    def fold(x):
        return x.reshape(grp, 8, tq)

    def scores(t, c, m):
        qrows = slice(t * tq, (t + 1) * tq)
        krows = pl.ds(pl.multiple_of(c * tq, tq), tq)
        d = c - (step * ATT_TILES + t)
        near = band_ref[0, jnp.clip(d + 1, 0, 2)]
        bias = jnp.where(jnp.abs(d) <= 1, near, jnp.where(d < 0, far_l, far_r))
        s1 = _dot_nt(k1_ref[krows, :], q1_ref[qrows, :] * head_mask) + bias
        s2 = _dot_nt(k2_ref[krows, :], q2_ref[qrows, :] * head_mask) + bias
        s1_scr[t][krows, :] = s1
        s2_scr[t][krows, :] = s2
        return (jnp.maximum(m[0], jnp.max(fold(s1), axis=0)),
                jnp.maximum(m[1], jnp.max(fold(s2), axis=0)))

    def col_max(m):
        return tuple(jnp.max(x, axis=0, keepdims=True) for x in m)

    def values(t, g, m):
        if isinstance(g, int):
            rows = slice(g * gkeys, (g + 1) * gkeys)
        else:
            rows = pl.ds(pl.multiple_of(g * gkeys, gkeys), gkeys)
        p1_scr[rows, :] = jnp.exp((s1_scr[t][rows, :] - m[0]).astype(BF16))
        p2_scr[rows, :] = jnp.exp((s2_scr[t][rows, :] - m[1]).astype(BF16))
        acc_scr[0] += _dot(vt_scr[g], p1_scr[rows, :])
        acc_scr[1] += _dot(vt_scr[g], p2_scr[rows, :])

    def finish(t):
        n1 = 1.0 / acc_scr[0, A_V:A_V + 1, :]
        n2 = lam_ref[0] / acc_scr[1, A_V:A_V + 1, :]
        ot = acc_scr[0, :A_V, :] * n1 - acc_scr[1, :A_V, :] * n2
        ms = jnp.mean(ot * ot, axis=0, keepdims=True)
        ot = ot * lax.rsqrt(ms + SUBLN_EPS) * (subg_ref[...] * (1.0 - lam_init))
        o_ref[t * tq:(t + 1) * tq, :] = ot.T.astype(o_ref.dtype)

    m = col_max(lax.fori_loop(0, nch, lambda c, mm: scores(0, c, mm), (neg, neg), unroll=min(nch, 8)))
    for t in range(ATT_TILES):
        acc_scr[...] = jnp.zeros_like(acc_scr)
        if t + 1 < ATT_TILES:
            def mixed(g, mm, t=t, m=m):
                values(t, g, m)
                for j in range(gch):
                    mm = scores(t + 1, g * gch + j, mm)
                return mm

            m_next = col_max(lax.fori_loop(0, ngrp, mixed, (neg, neg), unroll=min(ngrp, 2)))
        else:
            for g in range(ngrp):
                values(t, g, m)
        finish(t)
        if t + 1 < ATT_TILES:
            m = m_next


def diff_attention(proj, row_off, nbatch, seq, lam, far, band, subg_col, lam_init, tq):
    nq = seq // tq
    gch = 4
    tiles = ATT_TILES
    assert tq >= LANES and nq >= 3 and nq % gch == 0 and nq % tiles == 0 and row_off % seq == 0
    ns = nq // tiles
    boff = row_off // seq
    qoff = row_off // (tiles * tq)
    na = A_HEADS * A_QK // LANES

    def qspec(sec):
        return pl.BlockSpec((tiles * tq, LANES), lambda b, h, i: (qoff + b * ns + i, sec * na + h // 2))

    def kspec(sec):
        return pl.BlockSpec((seq, LANES), lambda b, h, i: (boff + b, sec * na + h // 2))

    smem = pl.BlockSpec(memory_space=pltpu.SMEM)
    return pl.pallas_call(
        functools.partial(_diffattn_kernel, seq, tq, lam_init),
        grid=(nbatch, A_HEADS, ns),
        in_specs=[smem, smem, qspec(0), qspec(1), kspec(2), kspec(3),
                  pl.BlockSpec((seq, A_V), lambda b, h, i: (boff + b, 4 * na + h)),
                  pl.BlockSpec((1, 3, tq, tq), lambda b, h, i: (h, 0, 0, 0)),
                  pl.BlockSpec((A_V, 1), lambda b, h, i: (0, 0))],
        out_specs=pl.BlockSpec((tiles * tq, A_V), lambda b, h, i: (b * ns + i, h)),
        out_shape=jax.ShapeDtypeStruct((nbatch * seq, A_HEADS * A_V), BF16),
        scratch_shapes=[pltpu.VMEM((nq // gch, A_V + SUM_ROWS, gch * tq), BF16),
                        pltpu.VMEM((seq, tq), BF16), pltpu.VMEM((seq, tq), BF16),
                        pltpu.VMEM((2, A_V + SUM_ROWS, tq), F32)]
                       + [pltpu.VMEM((seq, tq), F32)] * (2 * tiles),
        compiler_params=_params("parallel", "parallel", "arbitrary"),
        name="diff_attention",
    )(lam, far, proj, proj, proj, proj, proj, band, subg_col)


def _na_tables(na_tab):
    half = NA_WR // 2
    qr = np.arange(NA_QR)[:, None]
    kr = np.arange(NA_KR)[None, :]
    dr = np.full((3, NA_QR, NA_KR), -1, np.int64)
    for var, (rs, off) in enumerate([(np.maximum(qr - half, 0), NA_WR - 1),
                                     (qr, half - 1),
                                     (half + np.minimum(qr, half), -1)]):
        dr[var] = np.where((kr >= rs) & (kr < rs + NA_WR), kr - qr + off, -1)
    c = np.arange(GRID_W)[:, None]
    kc = np.arange(GRID_W)[None, :]
    cs = np.clip(c - NA_WC // 2, 0, GRID_W - NA_WC)
    col_ok = (kc >= cs) & (kc < cs + NA_WC)
    dc = jnp.asarray(np.where(col_ok, kc - c + NA_WC - 1, -1).astype(np.int8))[None, None]
    na = na_tab.astype(F32)
    cols = jnp.zeros((B_HEADS, 2 * NA_WR - 1, GRID_W, GRID_W), F32)
    for s in range(2 * NA_WC - 1):
        cols = jnp.where(dc == s, na[:, :, s][:, :, None, None], cols)
    onehot = (dr[..., None] == np.arange(2 * NA_WR - 1)).astype(np.float32)
    tab = jnp.einsum('vqkr,hrcl->hvqckl', jnp.asarray(onehot), cols, precision=lax.Precision.HIGHEST)
    ok = jnp.asarray(dr >= 0)[None, :, :, None, :, None] & jnp.asarray(col_ok)[None, None, None, :, None, :]
    tab = jnp.where(ok, tab, -1e30)
    return tab.reshape(B_HEADS, 3, NA_QR * GRID_W, NA_KR * GRID_W)


def _na_kernel(rows, q_ref, k_ref, v_ref, tab_ref, o_ref):
    scale = B_HD ** -0.5
    nq = NA_QR * GRID_W
    nk = NA_KR * GRID_W

    def body(j, carry):
        r0 = j * NA_QR
        kr0 = jnp.clip(r0 - NA_WR // 2, 0, rows - NA_KR)
        var = lax.shift_right_logical(r0 - kr0, 2)
        qs = pl.multiple_of(r0 * GRID_W, nq)
        ks = pl.multiple_of(kr0 * GRID_W, (NA_WR // 2) * GRID_W)
        s = _dot_nt(q_ref[pl.ds(qs, nq), :], k_ref[pl.ds(ks, nk), :]) * scale + tab_ref[0, var]
        p = jnp.exp(s - jnp.max(s, axis=-1, keepdims=True))
        p = (p / jnp.sum(p, axis=-1, keepdims=True)).astype(BF16)
        o_ref[pl.ds(qs, nq), :] = _dot(p, v_ref[pl.ds(ks, nk), :]).astype(o_ref.dtype)
        return carry

    lax.fori_loop(0, rows // NA_QR, body, 0, unroll=2)


def neighbourhood_attention(proj, row_off, nbatch, seq, tab):
    rows = seq // GRID_W
    assert rows >= NA_KR and rows % NA_QR == 0 and NA_WR == 8 and row_off % seq == 0
    boff = row_off // seq
    base = (4 * A_HEADS * A_QK + A_HEADS * A_V) // LANES
    nb = B_HEADS * B_HD // LANES

    def spec(sec):
        return pl.BlockSpec((seq, B_HD), lambda b, h: (boff + b, base + sec * nb + h))

    return pl.pallas_call(
        functools.partial(_na_kernel, rows),
        grid=(nbatch, B_HEADS),
        in_specs=[spec(0), spec(1), spec(2),
                  pl.BlockSpec((1, 3, NA_QR * GRID_W, NA_KR * GRID_W), lambda b, h: (h, 0, 0, 0))],
        out_specs=pl.BlockSpec((seq, B_HD), lambda b, h: (b, h)),
        out_shape=jax.ShapeDtypeStruct((nbatch * seq, B_HEADS * B_HD), BF16),
        compiler_params=_params("parallel", "parallel"),
        name="neighbourhood_attention",
    )(proj, proj, proj, tab)


def _dft_channel_tables():
    j = np.arange(C_GD)
    ang = 2.0 * np.pi * ((j[:, None] * j[None, :]) % C_GD) / C_GD
    s = C_GD ** -0.5
    return jnp.asarray(np.cos(ang) * s, BF16), jnp.asarray(np.sin(ang) * s, BF16)


def _dft_position_table(seq):
    r = 64
    k = jnp.arange(seq, dtype=I32)
    jh = jnp.arange(seq // r, dtype=I32)
    jl = jnp.arange(r, dtype=I32)
    step = 2.0 * np.pi / seq
    a = ((jh[:, None] * r * k[None, :]) % seq).astype(F32) * step
    b = ((jl[:, None] * k[None, :]) % seq).astype(F32) * step
    ca, sa = jnp.cos(a)[:, None, :], jnp.sin(a)[:, None, :]
    cb, sb = jnp.cos(b)[None, :, :], jnp.sin(b)[None, :, :]
    scale = seq ** -0.5
    c = ((ca * cb - sa * sb) * scale).reshape(seq, seq)
    s = ((sa * cb + ca * sb) * -scale).reshape(seq, seq)
    return jnp.concatenate([c, s], axis=1).astype(BF16)


def _fourier_channel_kernel(u_ref, cc_ref, sc_ref, o_ref):
    for g in range(C_GROUPS):
        sl = slice(g * C_GD, (g + 1) * C_GD)
        ug = u_ref[:, sl]
        o_ref[0, 0, :, sl] = _dot(ug, cc_ref[...]).astype(o_ref.dtype)
        o_ref[0, 1, :, sl] = _dot(ug, sc_ref[...]).astype(o_ref.dtype)


def _matmul_kernel(a_ref, b_ref, o_ref):
    o_ref[...] = _dot(a_ref[...], b_ref[0]).astype(o_ref.dtype)


def fourier_mix(proj, row_off, nbatch, seq, cc, sc, pos_tab):
    tm = 512
    assert row_off % tm == 0 and seq % tm == 0
    ns = seq // tm
    roff = row_off // tm
    v = pl.pallas_call(
        _fourier_channel_kernel,
        grid=(nbatch, ns),
        in_specs=[pl.BlockSpec((tm, C_WIDTH), lambda b, i: (roff + b * ns + i, 0)),
                  pl.BlockSpec((C_GD, C_GD), lambda b, i: (0, 0)),
                  pl.BlockSpec((C_GD, C_GD), lambda b, i: (0, 0))],
        out_specs=pl.BlockSpec((1, 2, tm, C_WIDTH), lambda b, i: (b, 0, i, 0)),
        out_shape=jax.ShapeDtypeStruct((nbatch, 2, seq, C_WIDTH), BF16),
        compiler_params=_params("parallel", "parallel"),
        name="fourier_channels",
    )(proj, cc, sc)
    v = v.reshape(nbatch, 2 * seq, C_WIDTH)
    tn = 512
    return pl.pallas_call(
        _matmul_kernel,
        grid=(ns, nbatch, C_WIDTH // tn),
        in_specs=[pl.BlockSpec((tm, 2 * seq), lambda i, b, j: (i, 0)),
                  pl.BlockSpec((1, 2 * seq, tn), lambda i, b, j: (b, 0, j))],
        out_specs=pl.BlockSpec((tm, tn), lambda i, b, j: (b * ns + i, j)),
        out_shape=jax.ShapeDtypeStruct((nbatch * seq, C_WIDTH), BF16),
        compiler_params=_params("parallel", "parallel", "parallel"),
        name="fourier_positions",
    )(pos_tab, v)


def _conv_kernel(bg_ref, cg_ref, hv_ref, cw_ref, cb_ref, o_ref):
    u = cg_ref[...].astype(F32) * hv_ref[...].astype(F32)
    seq = u.shape[0]
    row = lax.broadcasted_iota(I32, u.shape, 0)
    prev = jnp.where(row == 0, 0.0, pltpu.roll(u, 1, 0))
    nxt = jnp.where(row == seq - 1, 0.0, pltpu.roll(u, seq - 1, 0))
    y = prev * cw_ref[0:1, :] + u * cw_ref[1:2, :] + nxt * cw_ref[2:3, :] + cb_ref[...]
    o_ref[...] = (bg_ref[...].astype(F32) * y).astype(o_ref.dtype)


def short_conv(proj, row_off, nbatch, seq, cw, cb):
    tc = 256
    assert row_off % seq == 0
    boff = row_off // seq
    nct = D_WIDTH // tc
    base = C_WIDTH // tc

    def spec(sec):
        return pl.BlockSpec((seq, tc), lambda b, j: (boff + b, base + sec * nct + j))

    return pl.pallas_call(
        _conv_kernel,
        grid=(nbatch, nct),
        in_specs=[spec(0), spec(1), spec(2),
                  pl.BlockSpec((3, tc), lambda b, j: (0, j)),
                  pl.BlockSpec((1, tc), lambda b, j: (0, j))],
        out_specs=pl.BlockSpec((seq, tc), lambda b, j: (b, j)),
        out_shape=jax.ShapeDtypeStruct((nbatch * seq, D_WIDTH), BF16),
        compiler_params=_params("parallel", "parallel"),
        name="short_conv",
    )(proj, proj, proj, cw, cb.reshape(1, D_WIDTH))


def _route(logits, bias):
    s = 1.0 / (1.0 + jnp.exp(-logits))
    sel = s + bias
    sv = [s[j * N_GROUPS:(j + 1) * N_GROUPS] for j in range(EPG)]
    cv = [sel[j * N_GROUPS:(j + 1) * N_GROUPS] for j in range(EPG)]
    hi1, lo1 = jnp.maximum(cv[0], cv[1]), jnp.minimum(cv[0], cv[1])
    hi2, lo2 = jnp.maximum(cv[2], cv[3]), jnp.minimum(cv[2], cv[3])
    gscore = jnp.maximum(hi1, hi2) + jnp.maximum(jnp.minimum(hi1, hi2), jnp.maximum(lo1, lo2))
    gidx = lax.broadcasted_iota(I32, gscore.shape, 0).astype(F32)
    gmax = jnp.max(gscore, axis=0, keepdims=True)
    best = jnp.min(jnp.where(gscore == gmax, gidx, float(N_GROUPS)), axis=0, keepdims=True)
    onehot = gidx == best

    def pick(z):
        return jnp.sum(jnp.where(onehot, z, 0.0), axis=0, keepdims=True)

    c = [pick(z) for z in cv]
    w = [pick(z) for z in sv]
    i1, v1 = jnp.zeros_like(best), c[0]
    for j in range(1, EPG):
        better = c[j] > v1
        i1 = jnp.where(better, float(j), i1)
        v1 = jnp.where(better, c[j], v1)
    i2, v2 = jnp.full_like(best, -1.0), jnp.full_like(best, -jnp.inf)
    for j in range(EPG):
        better = (i1 != float(j)) & ((c[j] > v2) | (i2 < 0.0))
        i2 = jnp.where(better, float(j), i2)
        v2 = jnp.where(better, c[j], v2)
    lo = jnp.minimum(i1, i2)
    hi = jnp.maximum(i1, i2)

    def gate(idx):
        g = w[0]
        for j in range(1, EPG):
            g = jnp.where(idx == float(j), w[j], g)
        return g

    w_lo, w_hi = gate(lo), gate(hi)
    total = w_lo + w_hi
    pair = jnp.where(lo == 0.0, 0.0, jnp.where(lo == 1.0, 3.0, 5.0)) + (hi - lo - 1.0)
    cls = (best * float(N_PAIRS) + pair).astype(I32)
    return cls, w_lo / total, w_hi / total


def _outproj_router_kernel(n1, tb_ref, a1_ref, b1_ref, a2_ref, b2_ref, x_ref, g1_ref, sc_ref, sh_ref,
                           g_ref, w_ref, wrh_ref, wrl_ref, br_ref,
                           xo_ref, h_ref, cls_ref, wlo_ref, whi_ref):
    first = pl.program_id(0) < n1
    o1 = jnp.where(first, a1_ref[...], a2_ref[...])
    o2 = jnp.where(first, b1_ref[...], b2_ref[...])
    m = _dot(o1, w_ref[0]) + _dot(o2, w_ref[1])
    x = x_ref[...] + g1_ref[0, 0] * m
    xo_ref[...] = x
    h = _rms(x, g_ref[...], EPS) * (1.0 + sc_ref[0, 0]) + sh_ref[0, 0]
    h_ref[...] = h
    h_hi, h_lo = _split_bf16(h)
    logits = _dot(h_hi, wrh_ref[...]) + _dot(h_lo, wrh_ref[...]) + _dot(h_hi, wrl_ref[...])
    cls, w_lo, w_hi = _route(logits.T[:N_EXPERTS], br_ref[...])
    cls_ref[...] = cls
    wlo_ref[...] = w_lo
    whi_ref[...] = w_hi


def outproj_router(o1, o2, n_first, x, tile_batch, mod4, w_out, g, wr_hi, wr_lo, br):
    t, d = x.shape
    tm = ROUTE_TILE
    assert n_first % tm == 0
    n1 = n_first // tm
    half = w_out.shape[0] // 2

    def first(i, tb):
        return (jnp.minimum(i, n1 - 1), 0)

    def second(i, tb):
        return (jnp.maximum(i - n1, 0), 0)

    def piece(p):
        return pl.BlockSpec((1, 1, 1, d), lambda i, tb: (tb[i], p, 0, 0))

    row = pl.BlockSpec((tm, d), lambda i, tb: (i, 0))
    vec = pl.BlockSpec((1, tm), lambda i, tb: (0, i))
    full = lambda shape: pl.BlockSpec(shape, lambda i, tb: (0,) * len(shape))
    return pl.pallas_call(
        functools.partial(_outproj_router_kernel, n1),
        grid_spec=pltpu.PrefetchScalarGridSpec(
            num_scalar_prefetch=1, grid=(t // tm,),
            in_specs=[pl.BlockSpec((tm, half), first), pl.BlockSpec((tm, half), first),
                      pl.BlockSpec((tm, half), second), pl.BlockSpec((tm, half), second),
                      row, piece(2), piece(4), piece(3), full((1, d)),
                      full((2, half, d)), full((d, LANES)), full((d, LANES)),
                      full((N_EXPERTS, 1))],
            out_specs=[row, row, vec, vec, vec]),
        out_shape=[jax.ShapeDtypeStruct((t, d), F32), jax.ShapeDtypeStruct((t, d), F32),
                   jax.ShapeDtypeStruct((1, t), I32), jax.ShapeDtypeStruct((1, t), F32),
                   jax.ShapeDtypeStruct((1, t), F32)],
        compiler_params=_params("parallel"),
        name="outproj_router",
    )(tile_batch, o1[0], o2[0], o1[1], o2[1], x, mod4, mod4, mod4, g.reshape(1, d),
      w_out.reshape(2, half, d), wr_hi, wr_lo, br)


def _moe_kernel(bm, nsteps, ea_ref, eb_ref, nused_ref,
                src0_ref, src1_ref, src2_ref, dst_ref, h_ref, wa_ref, wb_ref,
                gua_ref, dna_ref, gub_ref, dnb_ref, y_ref, xbuf, ybuf, gsem, ssem):
    i = pl.program_id(0)
    n = nused_ref[0]
    gslot = i % 3
    sslot = i % 2
    ntok = y_ref.shape[0] - 2 * bm

    def gather_row(ids_ref, r, s):
        return pltpu.make_async_copy(h_ref.at[pl.ds(ids_ref[0, 0, r], 1), :],
                                     xbuf.at[s, pl.ds(r, 1), :], gsem.at[s])

    def gather_loop(ids_ref, s):
        def body(c, carry):
            for j in range(DMA_UNROLL):
                gather_row(ids_ref, c * DMA_UNROLL + j, s).start()
            return carry

        lax.fori_loop(0, bm // DMA_UNROLL, body, 0)

    def wait_rows(src, dst, sem):
        def body(c, carry):
            for _ in range(DMA_UNROLL):
                pltpu.make_async_copy(src.at[pl.ds(0, 1), :], dst.at[pl.ds(0, 1), :], sem).wait()
            return carry

        lax.fori_loop(0, bm // DMA_UNROLL, body, 0)

    def wait_gather(s):
        wait_rows(h_ref, xbuf.at[s], gsem.at[s])

    def wait_scatter(s):
        wait_rows(ybuf.at[s], y_ref, ssem.at[s])

    @pl.when(i == 0)
    def _():
        gather_loop(src0_ref, 0)
        gather_loop(src1_ref, 1)
        ybuf[0] = jnp.zeros((bm, ybuf.shape[2]), F32)
        spare = [pltpu.make_async_copy(ybuf.at[0], y_ref.at[pl.ds(ntok + k * bm, bm), :], ssem.at[k])
                 for k in range(2)]
        for c in spare:
            c.start()
        for c in spare:
            c.wait()

    @pl.when(i < n)
    def _():
        wait_gather(gslot)

        @pl.when(i >= 2)
        def _():
            wait_scatter(sslot)

    @pl.when(i < n)
    def _():
        x = xbuf[gslot].astype(BF16)

        def expert(gu_ref, dn_ref):
            gu = _dot(x, gu_ref[0])
            g, u = gu[:, :D_EXPERT], gu[:, D_EXPERT:]
            act = (g / (1.0 + jnp.exp(-g))) * u
            return _dot(act.astype(BF16), dn_ref[0])

        ybuf[sslot] = wa_ref[...] * expert(gua_ref, dna_ref) + wb_ref[...] * expert(gub_ref, dnb_ref)
        nxt = (i + 2) % 3
        for r in range(bm):
            gather_row(src2_ref, r, nxt).start()
        for r in range(bm):
            pltpu.make_async_copy(ybuf.at[sslot, pl.ds(r, 1), :],
                                  y_ref.at[pl.ds(dst_ref[0, 0, r], 1), :], ssem.at[sslot]).start(priority=1)

    @pl.when(i == nsteps - 1)
    def _():
        wait_gather(n % 3)
        wait_gather((n + 1) % 3)
        wait_scatter((n - 1) % 2)

        @pl.when(n >= 2)
        def _():
            wait_scatter(n % 2)


def moe_experts(h, row_src, row_dst, blk_a, blk_b, n_used, row_wa, row_wb, w_gu, w_dn):
    t, d = h.shape
    bm = MOE_BLOCK
    nrows = row_src.shape[0]
    nblk = nrows // bm
    col = pl.BlockSpec((bm, 1), lambda i, *_: (i, 0))
    ids = lambda f: pl.BlockSpec((1, 1, bm), lambda i, *_: (jnp.minimum(f(i), nblk - 1), 0, 0),
                                 memory_space=pltpu.SMEM)
    gu = lambda sel: pl.BlockSpec((1, d, 2 * D_EXPERT), lambda i, *s: (s[sel][i], 0, 0))
    dn = lambda sel: pl.BlockSpec((1, D_EXPERT, d), lambda i, *s: (s[sel][i], 0, 0))
    hbm = pl.BlockSpec(memory_space=pl.ANY)
    src3 = row_src.reshape(nblk, 1, bm)
    return pl.pallas_call(
        functools.partial(_moe_kernel, bm, nblk),
        grid_spec=pltpu.PrefetchScalarGridSpec(
            num_scalar_prefetch=3, grid=(nblk,),
            in_specs=[ids(lambda i: 0), ids(lambda i: 1), ids(lambda i: i + 2), ids(lambda i: i),
                      hbm, col, col, gu(0), dn(0), gu(1), dn(1)],
            out_specs=hbm,
            scratch_shapes=[pltpu.VMEM((3, bm, d), F32), pltpu.VMEM((2, bm, d), F32),
                            pltpu.SemaphoreType.DMA((3,)), pltpu.SemaphoreType.DMA((2,))]),
        out_shape=jax.ShapeDtypeStruct((t + 2 * bm, d), F32),
        compiler_params=_params("arbitrary"),
        name="moe_experts",
    )(blk_a, blk_b, n_used, src3, src3, src3, row_dst.reshape(nblk, 1, bm), h,
      row_wa.reshape(nrows, 1), row_wb.reshape(nrows, 1), w_gu, w_dn, w_gu, w_dn)


def _dispatch(cls, w_lo, w_hi, expert_base):
    t = cls.shape[0]
    bm = MOE_BLOCK
    nblk = -(-t // bm) + N_CLASSES
    nrows = nblk * bm
    iota = jnp.arange(t, dtype=I32)
    _, order, s_lo, s_hi = lax.sort((cls, iota, w_lo, w_hi), num_keys=1, is_stable=True)
    classes = jnp.arange(N_CLASSES, dtype=I32)
    counts = jnp.sum((cls[:, None] == classes[None, :]).astype(I32), axis=0)
    padded = (counts + bm - 1) // bm * bm
    pend = jnp.cumsum(padded)
    pstart = pend - padded
    start = jnp.cumsum(counts) - counts
    blk_start = jnp.arange(nblk, dtype=I32) * bm
    blk_cls = jnp.minimum(jnp.sum((pend[None, :] <= blk_start[:, None]).astype(I32), axis=-1),
                          N_CLASSES - 1)
    k0 = blk_start - pstart[blk_cls]
    k = k0[:, None] + jnp.arange(bm, dtype=I32)[None, :]
    valid = (k < counts[blk_cls][:, None]).reshape(nrows)
    src = jnp.clip(start[blk_cls][:, None] + k, 0, t - 1).reshape(nrows)
    tok = order[src]
    row_src = jnp.where(valid, tok, 0)
    spare = (t + (jnp.arange(nblk, dtype=I32) % 2)[:, None] * bm
             + jnp.arange(bm, dtype=I32)[None, :]).reshape(nrows)
    row_dst = jnp.where(valid, tok, spare)
    row_wa = jnp.where(valid, s_lo[src], 0.0)
    row_wb = jnp.where(valid, s_hi[src], 0.0)
    grp = blk_cls // N_PAIRS
    pair = blk_cls % N_PAIRS
    blk_a = expert_base + grp * EPG + jnp.asarray(PAIR_LO)[pair]
    blk_b = expert_base + grp * EPG + jnp.asarray(PAIR_HI)[pair]
    n_used = (pend[-1] // bm).reshape(1).astype(I32)
    return row_src, row_dst, row_wa, row_wb, blk_a.astype(I32), blk_b.astype(I32), n_used


def _final_kernel(tb_ref, x_ref, y_ref, g2_ref, g_ref, o_ref):
    x = x_ref[...] + g2_ref[0, 0] * y_ref[...]
    o_ref[...] = _rms(x, g_ref[...], EPS)


def final_norm(x, y, tile_batch, tile_off, nrows, mod4, g):
    d = x.shape[1]
    tm = ROW_TILE
    row = pl.BlockSpec((tm, d), lambda i, tb: (tile_off + i, 0))
    return pl.pallas_call(
        _final_kernel,
        grid_spec=pltpu.PrefetchScalarGridSpec(
            num_scalar_prefetch=1, grid=(nrows // tm,),
            in_specs=[row, row,
                      pl.BlockSpec((1, 1, 1, d), lambda i, tb: (tb[tile_off + i], 5, 0, 0)),
                      pl.BlockSpec((1, d), lambda i, tb: (0, 0))],
            out_specs=pl.BlockSpec((tm, d), lambda i, tb: (i, 0))),
        out_shape=jax.ShapeDtypeStruct((nrows, d), F32),
        compiler_params=_params("parallel"),
        name="final_norm",
    )(tile_batch, x, y, mod4, g.reshape(1, d))


def _tile_batch(tile, groups):
    out = []
    base = 0
    for nb, seq in groups:
        assert seq % tile == 0
        out.append(base + np.repeat(np.arange(nb), seq // tile))
        base += nb
    return jnp.asarray(np.concatenate(out), I32)


def kernel(x_prompt, x_sample, c_prompt, c_sample, w_ada, b_ada, norm_mix_g, norm_ffn_g, w_in_ab, w_out_ab, lambda_q1, lambda_k1, lambda_q2, lambda_k2, subln_g, na_bias, w_in_cd, conv_w, conv_b, w_out_cd, rel_bias, w_router, b_router, w_gate_up, w_down, final_norm_g):
    depth, d = norm_mix_g.shape
    groups = [(x_prompt.shape[0], x_prompt.shape[1]), (x_sample.shape[0], x_sample.shape[1])]
    rows = [nb * seq for nb, seq in groups]
    offs = [0, rows[0]]
    t = rows[0] + rows[1]
    nb_all = groups[0][0] + groups[1][0]
    assert nb_all <= NB_PAD

    x = (x_prompt.reshape(rows[0], d), x_sample.reshape(rows[1], d))
    c_all = jnp.concatenate([c_prompt, c_sample, jnp.zeros((NB_PAD - nb_all, d), F32)], axis=0)
    mod = ada_mod(c_all, w_ada, b_ada).reshape(depth, NB_PAD, 6, 1, d)
    tb_row = _tile_batch(ROW_TILE, groups)
    tb_route = _tile_batch(ROUTE_TILE, groups)
    tb_proj = _tile_batch(PROJ_TILE, groups)

    wr = w_router.astype(F32).reshape(d, N_GROUPS, EPG).transpose(0, 2, 1).reshape(d, N_EXPERTS)
    wr_hi, wr_lo = _split_bf16(jnp.pad(wr, ((0, 0), (0, LANES - N_EXPERTS))))
    br = b_router.astype(F32).reshape(N_GROUPS, EPG).T.reshape(N_EXPERTS, 1)

    cc, sc = _dft_channel_tables()
    w_gu = w_gate_up.astype(BF16).reshape((depth * N_EXPERTS,) + w_gate_up.shape[2:])
    w_dn = w_down.astype(BF16).reshape((depth * N_EXPERTS,) + w_down.shape[2:])
    y = None
    for l in range(depth):
        i = l // 2
        mod4 = mod[l]
        if l % 2 == 0:
            w_in, w_out = w_in_ab[i], w_out_ab[i]
        else:
            w_in, w_out = w_in_cd[i], w_out_cd[i]
        proj, x = norm_matmul(x, tb_proj, mod4, 1, 0, norm_mix_g[l], w_in.astype(BF16),
                              y=y, g2_mod=mod[l - 1] if l else None, g2_piece=5)
        if l % 2 == 0:
            lam_init = 0.8 - 0.6 * math.exp(-0.3 * l)
            lam = (jnp.exp(jnp.sum(lambda_q1[i].astype(F32) * lambda_k1[i].astype(F32)))
                   - jnp.exp(jnp.sum(lambda_q2[i].astype(F32) * lambda_k2[i].astype(F32)))
                   + lam_init).reshape(1)
            rb = rel_bias.astype(F32)
            far = jnp.stack([rb[T5_BUCKETS // 2 - 1], rb[T5_BUCKETS - 1]], axis=1)
            na_tab = _na_tables(na_bias[i])
            subg = subln_g[i].astype(F32).reshape(A_V, 1)
            o1, o2, bands = [], [], {}
            for (nb, seq), off in zip(groups, offs):
                tq = min(ATT_TQ, seq // 4)
                if tq not in bands:
                    bands[tq] = _t5_band_tables(rel_bias, tq)
                o1.append(diff_attention(proj, off, nb, seq, lam, far, bands[tq], subg, lam_init, tq))
                o2.append(neighbourhood_attention(proj, off, nb, seq, na_tab))
        else:
            o1, o2 = [], []
            for (nb, seq), off in zip(groups, offs):
                o1.append(fourier_mix(proj, off, nb, seq, cc, sc, _dft_position_table(seq)))
                o2.append(short_conv(proj, off, nb, seq, conv_w[i].astype(F32), conv_b[i].astype(F32)))
        x, h2, cls, w_lo, w_hi = outproj_router(o1, o2, rows[0], x, tb_route, mod4,
                                                w_out.astype(BF16), norm_ffn_g[l], wr_hi, wr_lo, br)
        row_src, row_dst, row_wa, row_wb, blk_a, blk_b, n_used = _dispatch(
            cls[0], w_lo[0], w_hi[0], l * N_EXPERTS)
        y = moe_experts(h2, row_src, row_dst, blk_a, blk_b, n_used, row_wa, row_wb, w_gu, w_dn)

    mod4 = mod[depth - 1]
    outs = []
    for (nb, seq), off, n in zip(groups, offs, rows):
        o = final_norm(x, y, tb_row, off // ROW_TILE, n, mod4, final_norm_g)
        outs.append(o.reshape(nb, seq, d))
    return tuple(outs)
```

```python
import functools
import math

import numpy as np
import jax
import jax.numpy as jnp
from jax import lax
from jax.experimental import pallas as pl
from jax.experimental.pallas import tpu as pltpu

F32 = jnp.float32
BF16 = jnp.bfloat16
I32 = jnp.int32

GRID_W = 64
A_HEADS = 8
A_QK = 64
A_V = 128
SUBLN_EPS = 1e-5
B_HEADS = 8
B_HD = 128
NA_WR = 8
NA_WC = 16
NA_QR = 8
NA_KR = 16
C_GROUPS = 4
C_GD = 256
C_WIDTH = 1024
D_WIDTH = 1024
T5_BUCKETS = 32
T5_MAX_DIST = 128
N_EXPERTS = 32
N_GROUPS = 8
EPG = 4
D_EXPERT = 512
EPS = 1e-6
N_PAIRS = 6
N_CLASSES = N_GROUPS * N_PAIRS
PAIR_LO = np.array([0, 0, 0, 1, 1, 2], np.int32)
PAIR_HI = np.array([1, 2, 3, 2, 3, 3], np.int32)

LANES = 128
VMEM_LIMIT = 56 * 1024 * 1024

NB_PAD = 16
ROW_TILE = 512
PROJ_TILE = 256
ROUTE_TILE = 256
MOE_BLOCK = 256
ATT_TQ = 256
ATT_TILES = 4
DMA_UNROLL = 8
SUM_ROWS = 16


def _dot(a, b):
    return jnp.dot(a, b, preferred_element_type=F32)


def _dot_nt(a, b):
    return lax.dot_general(a, b, (((1,), (1,)), ((), ())), preferred_element_type=F32)


def _split_bf16(x):
    hi = x.astype(BF16)
    lo = (x - hi.astype(F32)).astype(BF16)
    return hi, lo


def _params(*sem):
    return pltpu.CompilerParams(dimension_semantics=sem, vmem_limit_bytes=VMEM_LIMIT)


def _ada_kernel(c_ref, w_ref, b_ref, o_ref):
    c = c_ref[...]
    cs = c / (1.0 + jnp.exp(-c))
    cs_hi, cs_lo = _split_bf16(cs)
    w_hi, w_lo = _split_bf16(w_ref[0])
    acc = _dot(cs_hi, w_hi) + _dot(cs_lo, w_hi) + _dot(cs_hi, w_lo)
    o_ref[0] = acc + b_ref[0]


def ada_mod(c_all, w_ada, b_ada):
    depth, d, n = w_ada.shape
    tn = 1024
    return pl.pallas_call(
        _ada_kernel,
        grid=(depth, n // tn),
        in_specs=[
            pl.BlockSpec((NB_PAD, d), lambda l, j: (0, 0)),
            pl.BlockSpec((1, d, tn), lambda l, j: (l, 0, j)),
            pl.BlockSpec((1, 1, tn), lambda l, j: (l, 0, j)),
        ],
        out_specs=pl.BlockSpec((1, NB_PAD, tn), lambda l, j: (l, 0, j)),
        out_shape=jax.ShapeDtypeStruct((depth, NB_PAD, n), F32),
        compiler_params=_params("parallel", "parallel"),
        name="ada_mod",
    )(c_all, w_ada, b_ada.reshape(depth, 1, n))


def _rms(x, g, eps):
    return x * lax.rsqrt(jnp.mean(x * x, axis=-1, keepdims=True) + eps) * g


def _norm_matmul_kernel(n_first, tn, tb_ref, *refs):
    if n_first is None:
        x_ref, y_ref, g2_ref, sc_ref, sh_ref, g_ref, w_ref, o_ref, x2_ref, h_scr = refs
        x = x_ref[...] + g2_ref[0, 0] * y_ref[...]
    else:
        xa_ref, xb_ref, sc_ref, sh_ref, g_ref, w_ref, o_ref, x2_ref, h_scr = refs
        x = jnp.where(pl.program_id(0) < n_first, xa_ref[...], xb_ref[...])
    x2_ref[...] = x
    h = _rms(x, g_ref[...], EPS) * (1.0 + sc_ref[0, 0]) + sh_ref[0, 0]
    h_scr[...] = h.astype(BF16)
    for j in range(w_ref.shape[1] // tn):
        cols = slice(j * tn, (j + 1) * tn)
        o_ref[:, cols] = _dot(h_scr[...], w_ref[:, cols]).astype(o_ref.dtype)


def norm_matmul(x, tile_batch, mod4, sc_piece, sh_piece, g, w, y=None, g2_mod=None, g2_piece=None):
    n = w.shape[1]
    tm, tn = PROJ_TILE, 1024

    def piece(p):
        return pl.BlockSpec((1, 1, 1, d), lambda i, tb: (tb[i], p, 0, 0))

    if y is None:
        xa, xb = x
        d = xa.shape[1]
        t = xa.shape[0] + xb.shape[0]
        assert xa.shape[0] % tm == 0
        n_first = xa.shape[0] // tm
        in_specs = [pl.BlockSpec((tm, d), lambda i, tb: (jnp.minimum(i, n_first - 1), 0)),
                    pl.BlockSpec((tm, d), lambda i, tb: (jnp.maximum(i - n_first, 0), 0))]
        args = [xa, xb]
    else:
        t, d = x.shape
        n_first = None
        in_specs = [pl.BlockSpec((tm, d), lambda i, tb: (i, 0)),
                    pl.BlockSpec((tm, d), lambda i, tb: (i, 0)), piece(g2_piece)]
        args = [x, y, g2_mod]
    in_specs += [piece(sc_piece), piece(sh_piece),
                 pl.BlockSpec((1, d), lambda i, tb: (0, 0)),
                 pl.BlockSpec((d, n), lambda i, tb: (0, 0), pipeline_mode=pl.Buffered(1))]
    args += [mod4, mod4, g.reshape(1, d), w]
    return pl.pallas_call(
        functools.partial(_norm_matmul_kernel, n_first, tn),
        grid_spec=pltpu.PrefetchScalarGridSpec(
            num_scalar_prefetch=1, grid=(t // tm,),
            in_specs=in_specs,
            out_specs=[pl.BlockSpec((tm, n), lambda i, tb: (i, 0)),
                       pl.BlockSpec((tm, d), lambda i, tb: (i, 0))],
            scratch_shapes=[pltpu.VMEM((tm, d), BF16)]),
        out_shape=[jax.ShapeDtypeStruct((t, n), BF16), jax.ShapeDtypeStruct((t, d), F32)],
        compiler_params=_params("parallel"),
        name="norm_matmul",
    )(tile_batch, *args)


def _t5_bucket_np(rel):
    half = T5_BUCKETS // 2
    max_exact = half // 2
    sign = (rel > 0).astype(np.int64) * half
    n = np.abs(rel)
    nf = np.maximum(n, 1).astype(np.float64)
    large = max_exact + (np.log(nf / max_exact) / math.log(T5_MAX_DIST / max_exact)
                         * (half - max_exact)).astype(np.int64)
    large = np.minimum(large, half - 1)
    return (sign + np.where(n < max_exact, n, large)).astype(np.int32)


def _t5_band_tables(rel_bias, tq):
    d = np.arange(-1, 2)[:, None, None]
    kk = np.arange(tq)[None, :, None]
    qq = np.arange(tq)[None, None, :]
    bucket = jnp.asarray(_t5_bucket_np(d * tq + kk - qq)[None].astype(np.int8))
    rb = rel_bias.astype(F32).T[:, :, None, None, None]
    tab = jnp.zeros((A_HEADS, 3, tq, tq), F32)
    for b in range(T5_BUCKETS):
        tab = jnp.where(bucket == b, rb[:, b], tab)
    return tab


def _diffattn_kernel(seq, tq, lam_init, lam_ref, far_ref, q1_ref, q2_ref, k1_ref, k2_ref, v_ref,
                     band_ref, subg_ref, o_ref, vt_scr, p1_scr, p2_scr, acc_scr, *s_scr):
    s1_scr, s2_scr = s_scr[0::2], s_scr[1::2]
    h = pl.program_id(1)
    step = pl.program_id(2)
    nch = seq // tq
    grp = tq // 8
    ngrp, _, gkeys = vt_scr.shape
    gch = gkeys // tq

    @pl.when(step == 0)
    def _():
        for g in range(ngrp):
            vt_scr[g, :A_V, :] = v_ref[g * gkeys:(g + 1) * gkeys, :].astype(F32).T.astype(BF16)
            vt_scr[g, A_V:, :] = jnp.ones((SUM_ROWS, gkeys), BF16)

    lane = lax.broadcasted_iota(I32, (1, LANES), 1)
    head_mask = jnp.where((lane // A_QK) == (h % 2), A_QK ** -0.5, 0.0).astype(BF16)
    far_l, far_r = far_ref[h, 0], far_ref[h, 1]
    neg = jnp.full((8, tq), -jnp.inf, F32)

    def fold(x):
        return x.reshape(grp, 8, tq)

    def scores(t, c, m):
        qrows = slice(t * tq, (t + 1) * tq)
        krows = pl.ds(pl.multiple_of(c * tq, tq), tq)
        d = c - (step * ATT_TILES + t)
        near = band_ref[0, jnp.clip(d + 1, 0, 2)]
        bias = jnp.where(jnp.abs(d) <= 1, near, jnp.where(d < 0, far_l, far_r))
        s1 = _dot_nt(k1_ref[krows, :], q1_ref[qrows, :] * head_mask) + bias
        s2 = _dot_nt(k2_ref[krows, :], q2_ref[qrows, :] * head_mask) + bias
        s1_scr[t][krows, :] = s1
        s2_scr[t][krows, :] = s2
        return (jnp.maximum(m[0], jnp.max(fold(s1), axis=0)),
                jnp.maximum(m[1], jnp.max(fold(s2), axis=0)))

    def col_max(m):
        return tuple(jnp.max(x, axis=0, keepdims=True) for x in m)

    def values(t, g, m):
        if isinstance(g, int):
            rows = slice(g * gkeys, (g + 1) * gkeys)
        else:
            rows = pl.ds(pl.multiple_of(g * gkeys, gkeys), gkeys)
        p1_scr[rows, :] = jnp.exp((s1_scr[t][rows, :] - m[0]).astype(BF16))
        p2_scr[rows, :] = jnp.exp((s2_scr[t][rows, :] - m[1]).astype(BF16))
        acc_scr[0] += _dot(vt_scr[g], p1_scr[rows, :])
        acc_scr[1] += _dot(vt_scr[g], p2_scr[rows, :])

    def finish(t):
        n1 = 1.0 / acc_scr[0, A_V:A_V + 1, :]
        n2 = lam_ref[0] / acc_scr[1, A_V:A_V + 1, :]
        ot = acc_scr[0, :A_V, :] * n1 - acc_scr[1, :A_V, :] * n2
        ms = jnp.mean(ot * ot, axis=0, keepdims=True)
        ot = ot * lax.rsqrt(ms + SUBLN_EPS) * (subg_ref[...] * (1.0 - lam_init))
        o_ref[t * tq:(t + 1) * tq, :] = ot.T.astype(o_ref.dtype)

    m = col_max(lax.fori_loop(0, nch, lambda c, mm: scores(0, c, mm), (neg, neg), unroll=min(nch, 8)))
    for t in range(ATT_TILES):
        acc_scr[...] = jnp.zeros_like(acc_scr)
        if t + 1 < ATT_TILES:
            def mixed(g, mm, t=t, m=m):
                values(t, g, m)
                for j in range(gch):
                    mm = scores(t + 1, g * gch + j, mm)
                return mm

            m_next = col_max(lax.fori_loop(0, ngrp, mixed, (neg, neg), unroll=min(ngrp, 2)))
        else:
            for g in range(ngrp):
                values(t, g, m)
        finish(t)
        if t + 1 < ATT_TILES:
            m = m_next


def diff_attention(proj, row_off, nbatch, seq, lam, far, band, subg_col, lam_init, tq):
    nq = seq // tq
    gch = 4
    tiles = ATT_TILES
    assert tq >= LANES and nq >= 3 and nq % gch == 0 and nq % tiles == 0 and row_off % seq == 0
    ns = nq // tiles
    boff = row_off // seq
    qoff = row_off // (tiles * tq)
    na = A_HEADS * A_QK // LANES

    def qspec(sec):
        return pl.BlockSpec((tiles * tq, LANES), lambda b, h, i: (qoff + b * ns + i, sec * na + h // 2))

    def kspec(sec):
        return pl.BlockSpec((seq, LANES), lambda b, h, i: (boff + b, sec * na + h // 2))

    smem = pl.BlockSpec(memory_space=pltpu.SMEM)
    return pl.pallas_call(
        functools.partial(_diffattn_kernel, seq, tq, lam_init),
        grid=(nbatch, A_HEADS, ns),
        in_specs=[smem, smem, qspec(0), qspec(1), kspec(2), kspec(3),
                  pl.BlockSpec((seq, A_V), lambda b, h, i: (boff + b, 4 * na + h)),
                  pl.BlockSpec((1, 3, tq, tq), lambda b, h, i: (h, 0, 0, 0)),
                  pl.BlockSpec((A_V, 1), lambda b, h, i: (0, 0))],
        out_specs=pl.BlockSpec((tiles * tq, A_V), lambda b, h, i: (b * ns + i, h)),
        out_shape=jax.ShapeDtypeStruct((nbatch * seq, A_HEADS * A_V), BF16),
        scratch_shapes=[pltpu.VMEM((nq // gch, A_V + SUM_ROWS, gch * tq), BF16),
                        pltpu.VMEM((seq, tq), BF16), pltpu.VMEM((seq, tq), BF16),
                        pltpu.VMEM((2, A_V + SUM_ROWS, tq), F32)]
                       + [pltpu.VMEM((seq, tq), F32)] * (2 * tiles),
        compiler_params=_params("parallel", "parallel", "arbitrary"),
        name="diff_attention",
    )(lam, far, proj, proj, proj, proj, proj, band, subg_col)


def _na_tables(na_tab):
    half = NA_WR // 2
    qr = np.arange(NA_QR)[:, None]
    kr = np.arange(NA_KR)[None, :]
    dr = np.full((3, NA_QR, NA_KR), -1, np.int64)
    for var, (rs, off) in enumerate([(np.maximum(qr - half, 0), NA_WR - 1),
                                     (qr, half - 1),
                                     (half + np.minimum(qr, half), -1)]):
        dr[var] = np.where((kr >= rs) & (kr < rs + NA_WR), kr - qr + off, -1)
    c = np.arange(GRID_W)[:, None]
    kc = np.arange(GRID_W)[None, :]
    cs = np.clip(c - NA_WC // 2, 0, GRID_W - NA_WC)
    col_ok = (kc >= cs) & (kc < cs + NA_WC)
    dc = jnp.asarray(np.where(col_ok, kc - c + NA_WC - 1, -1).astype(np.int8))[None, None]
    na = na_tab.astype(F32)
    cols = jnp.zeros((B_HEADS, 2 * NA_WR - 1, GRID_W, GRID_W), F32)
    for s in range(2 * NA_WC - 1):
        cols = jnp.where(dc == s, na[:, :, s][:, :, None, None], cols)
    onehot = (dr[..., None] == np.arange(2 * NA_WR - 1)).astype(np.float32)
    tab = jnp.einsum('vqkr,hrcl->hvqckl', jnp.asarray(onehot), cols, precision=lax.Precision.HIGHEST)
    ok = jnp.asarray(dr >= 0)[None, :, :, None, :, None] & jnp.asarray(col_ok)[None, None, None, :, None, :]
    tab = jnp.where(ok, tab, -1e30)
    return tab.reshape(B_HEADS, 3, NA_QR * GRID_W, NA_KR * GRID_W)


def _na_kernel(rows, q_ref, k_ref, v_ref, tab_ref, o_ref):
    scale = B_HD ** -0.5
    nq = NA_QR * GRID_W
    nk = NA_KR * GRID_W

    def body(j, carry):
        r0 = j * NA_QR
        kr0 = jnp.clip(r0 - NA_WR // 2, 0, rows - NA_KR)
        var = lax.shift_right_logical(r0 - kr0, 2)
        qs = pl.multiple_of(r0 * GRID_W, nq)
        ks = pl.multiple_of(kr0 * GRID_W, (NA_WR // 2) * GRID_W)
        s = _dot_nt(q_ref[pl.ds(qs, nq), :], k_ref[pl.ds(ks, nk), :]) * scale + tab_ref[0, var]
        p = jnp.exp(s - jnp.max(s, axis=-1, keepdims=True))
        p = (p / jnp.sum(p, axis=-1, keepdims=True)).astype(BF16)
        o_ref[pl.ds(qs, nq), :] = _dot(p, v_ref[pl.ds(ks, nk), :]).astype(o_ref.dtype)
        return carry

    lax.fori_loop(0, rows // NA_QR, body, 0, unroll=2)


def neighbourhood_attention(proj, row_off, nbatch, seq, tab):
    rows = seq // GRID_W
    assert rows >= NA_KR and rows % NA_QR == 0 and NA_WR == 8 and row_off % seq == 0
    boff = row_off // seq
    base = (4 * A_HEADS * A_QK + A_HEADS * A_V) // LANES
    nb = B_HEADS * B_HD // LANES

    def spec(sec):
        return pl.BlockSpec((seq, B_HD), lambda b, h: (boff + b, base + sec * nb + h))

    return pl.pallas_call(
        functools.partial(_na_kernel, rows),
        grid=(nbatch, B_HEADS),
        in_specs=[spec(0), spec(1), spec(2),
                  pl.BlockSpec((1, 3, NA_QR * GRID_W, NA_KR * GRID_W), lambda b, h: (h, 0, 0, 0))],
        out_specs=pl.BlockSpec((seq, B_HD), lambda b, h: (b, h)),
        out_shape=jax.ShapeDtypeStruct((nbatch * seq, B_HEADS * B_HD), BF16),
        compiler_params=_params("parallel", "parallel"),
        name="neighbourhood_attention",
    )(proj, proj, proj, tab)


def _dft_channel_tables():
    j = np.arange(C_GD)
    ang = 2.0 * np.pi * ((j[:, None] * j[None, :]) % C_GD) / C_GD
    s = C_GD ** -0.5
    return jnp.asarray(np.cos(ang) * s, BF16), jnp.asarray(np.sin(ang) * s, BF16)


def _dft_position_table(seq):
    r = 64
    k = jnp.arange(seq, dtype=I32)
    jh = jnp.arange(seq // r, dtype=I32)
    jl = jnp.arange(r, dtype=I32)
    step = 2.0 * np.pi / seq
    a = ((jh[:, None] * r * k[None, :]) % seq).astype(F32) * step
    b = ((jl[:, None] * k[None, :]) % seq).astype(F32) * step
    ca, sa = jnp.cos(a)[:, None, :], jnp.sin(a)[:, None, :]
    cb, sb = jnp.cos(b)[None, :, :], jnp.sin(b)[None, :, :]
    scale = seq ** -0.5
    c = ((ca * cb - sa * sb) * scale).reshape(seq, seq)
    s = ((sa * cb + ca * sb) * -scale).reshape(seq, seq)
    return jnp.concatenate([c, s], axis=1).astype(BF16)


def _fourier_channel_kernel(u_ref, cc_ref, sc_ref, o_ref):
    for g in range(C_GROUPS):
        sl = slice(g * C_GD, (g + 1) * C_GD)
        ug = u_ref[:, sl]
        o_ref[0, 0, :, sl] = _dot(ug, cc_ref[...]).astype(o_ref.dtype)
        o_ref[0, 1, :, sl] = _dot(ug, sc_ref[...]).astype(o_ref.dtype)


def _matmul_kernel(a_ref, b_ref, o_ref):
    o_ref[...] = _dot(a_ref[...], b_ref[0]).astype(o_ref.dtype)


def fourier_mix(proj, row_off, nbatch, seq, cc, sc, pos_tab):
    tm = 512
    assert row_off % tm == 0 and seq % tm == 0
    ns = seq // tm
    roff = row_off // tm
    v = pl.pallas_call(
        _fourier_channel_kernel,
        grid=(nbatch, ns),
        in_specs=[pl.BlockSpec((tm, C_WIDTH), lambda b, i: (roff + b * ns + i, 0)),
                  pl.BlockSpec((C_GD, C_GD), lambda b, i: (0, 0)),
                  pl.BlockSpec((C_GD, C_GD), lambda b, i: (0, 0))],
        out_specs=pl.BlockSpec((1, 2, tm, C_WIDTH), lambda b, i: (b, 0, i, 0)),
        out_shape=jax.ShapeDtypeStruct((nbatch, 2, seq, C_WIDTH), BF16),
        compiler_params=_params("parallel", "parallel"),
        name="fourier_channels",
    )(proj, cc, sc)
    v = v.reshape(nbatch, 2 * seq, C_WIDTH)
    tn = 512
    return pl.pallas_call(
        _matmul_kernel,
        grid=(ns, nbatch, C_WIDTH // tn),
        in_specs=[pl.BlockSpec((tm, 2 * seq), lambda i, b, j: (i, 0)),
                  pl.BlockSpec((1, 2 * seq, tn), lambda i, b, j: (b, 0, j))],
        out_specs=pl.BlockSpec((tm, tn), lambda i, b, j: (b * ns + i, j)),
        out_shape=jax.ShapeDtypeStruct((nbatch * seq, C_WIDTH), BF16),
        compiler_params=_params("parallel", "parallel", "parallel"),
        name="fourier_positions",
    )(pos_tab, v)


def _conv_kernel(bg_ref, cg_ref, hv_ref, cw_ref, cb_ref, o_ref):
    u = cg_ref[...].astype(F32) * hv_ref[...].astype(F32)
    seq = u.shape[0]
    row = lax.broadcasted_iota(I32, u.shape, 0)
    prev = jnp.where(row == 0, 0.0, pltpu.roll(u, 1, 0))
    nxt = jnp.where(row == seq - 1, 0.0, pltpu.roll(u, seq - 1, 0))
    y = prev * cw_ref[0:1, :] + u * cw_ref[1:2, :] + nxt * cw_ref[2:3, :] + cb_ref[...]
    o_ref[...] = (bg_ref[...].astype(F32) * y).astype(o_ref.dtype)


def short_conv(proj, row_off, nbatch, seq, cw, cb):
    tc = 256
    assert row_off % seq == 0
    boff = row_off // seq
    nct = D_WIDTH // tc
    base = C_WIDTH // tc

    def spec(sec):
        return pl.BlockSpec((seq, tc), lambda b, j: (boff + b, base + sec * nct + j))

    return pl.pallas_call(
        _conv_kernel,
        grid=(nbatch, nct),
        in_specs=[spec(0), spec(1), spec(2),
                  pl.BlockSpec((3, tc), lambda b, j: (0, j)),
                  pl.BlockSpec((1, tc), lambda b, j: (0, j))],
        out_specs=pl.BlockSpec((seq, tc), lambda b, j: (b, j)),
        out_shape=jax.ShapeDtypeStruct((nbatch * seq, D_WIDTH), BF16),
        compiler_params=_params("parallel", "parallel"),
        name="short_conv",
    )(proj, proj, proj, cw, cb.reshape(1, D_WIDTH))


def _route(logits, bias):
    s = 1.0 / (1.0 + jnp.exp(-logits))
    sel = s + bias
    sv = [s[j * N_GROUPS:(j + 1) * N_GROUPS] for j in range(EPG)]
    cv = [sel[j * N_GROUPS:(j + 1) * N_GROUPS] for j in range(EPG)]
    hi1, lo1 = jnp.maximum(cv[0], cv[1]), jnp.minimum(cv[0], cv[1])
    hi2, lo2 = jnp.maximum(cv[2], cv[3]), jnp.minimum(cv[2], cv[3])
    gscore = jnp.maximum(hi1, hi2) + jnp.maximum(jnp.minimum(hi1, hi2), jnp.maximum(lo1, lo2))
    gidx = lax.broadcasted_iota(I32, gscore.shape, 0).astype(F32)
    gmax = jnp.max(gscore, axis=0, keepdims=True)
    best = jnp.min(jnp.where(gscore == gmax, gidx, float(N_GROUPS)), axis=0, keepdims=True)
    onehot = gidx == best

    def pick(z):
        return jnp.sum(jnp.where(onehot, z, 0.0), axis=0, keepdims=True)

    c = [pick(z) for z in cv]
    w = [pick(z) for z in sv]
    i1, v1 = jnp.zeros_like(best), c[0]
    for j in range(1, EPG):
        better = c[j] > v1
        i1 = jnp.where(better, float(j), i1)
        v1 = jnp.where(better, c[j], v1)
    i2, v2 = jnp.full_like(best, -1.0), jnp.full_like(best, -jnp.inf)
    for j in range(EPG):
        better = (i1 != float(j)) & ((c[j] > v2) | (i2 < 0.0))
        i2 = jnp.where(better, float(j), i2)
        v2 = jnp.where(better, c[j], v2)
    lo = jnp.minimum(i1, i2)
    hi = jnp.maximum(i1, i2)

    def gate(idx):
        g = w[0]
        for j in range(1, EPG):
            g = jnp.where(idx == float(j), w[j], g)
        return g

    w_lo, w_hi = gate(lo), gate(hi)
    total = w_lo + w_hi
    pair = jnp.where(lo == 0.0, 0.0, jnp.where(lo == 1.0, 3.0, 5.0)) + (hi - lo - 1.0)
    cls = (best * float(N_PAIRS) + pair).astype(I32)
    return cls, w_lo / total, w_hi / total


def _outproj_router_kernel(n1, tb_ref, a1_ref, b1_ref, a2_ref, b2_ref, x_ref, g1_ref, sc_ref, sh_ref,
                           g_ref, w_ref, wrh_ref, wrl_ref, br_ref,
                           xo_ref, h_ref, cls_ref, wlo_ref, whi_ref):
    first = pl.program_id(0) < n1
    o1 = jnp.where(first, a1_ref[...], a2_ref[...])
    o2 = jnp.where(first, b1_ref[...], b2_ref[...])
    m = _dot(o1, w_ref[0]) + _dot(o2, w_ref[1])
    x = x_ref[...] + g1_ref[0, 0] * m
    xo_ref[...] = x
    h = _rms(x, g_ref[...], EPS) * (1.0 + sc_ref[0, 0]) + sh_ref[0, 0]
    h_ref[...] = h
    h_hi, h_lo = _split_bf16(h)
    logits = _dot(h_hi, wrh_ref[...]) + _dot(h_lo, wrh_ref[...]) + _dot(h_hi, wrl_ref[...])
    cls, w_lo, w_hi = _route(logits.T[:N_EXPERTS], br_ref[...])
    cls_ref[...] = cls
    wlo_ref[...] = w_lo
    whi_ref[...] = w_hi


def outproj_router(o1, o2, n_first, x, tile_batch, mod4, w_out, g, wr_hi, wr_lo, br):
    t, d = x.shape
    tm = ROUTE_TILE
    assert n_first % tm == 0
    n1 = n_first // tm
    half = w_out.shape[0] // 2

    def first(i, tb):
        return (jnp.minimum(i, n1 - 1), 0)

    def second(i, tb):
        return (jnp.maximum(i - n1, 0), 0)

    def piece(p):
        return pl.BlockSpec((1, 1, 1, d), lambda i, tb: (tb[i], p, 0, 0))

    row = pl.BlockSpec((tm, d), lambda i, tb: (i, 0))
    vec = pl.BlockSpec((1, tm), lambda i, tb: (0, i))
    full = lambda shape: pl.BlockSpec(shape, lambda i, tb: (0,) * len(shape))
    return pl.pallas_call(
        functools.partial(_outproj_router_kernel, n1),
        grid_spec=pltpu.PrefetchScalarGridSpec(
            num_scalar_prefetch=1, grid=(t // tm,),
            in_specs=[pl.BlockSpec((tm, half), first), pl.BlockSpec((tm, half), first),
                      pl.BlockSpec((tm, half), second), pl.BlockSpec((tm, half), second),
                      row, piece(2), piece(4), piece(3), full((1, d)),
                      full((2, half, d)), full((d, LANES)), full((d, LANES)),
                      full((N_EXPERTS, 1))],
            out_specs=[row, row, vec, vec, vec]),
        out_shape=[jax.ShapeDtypeStruct((t, d), F32), jax.ShapeDtypeStruct((t, d), F32),
                   jax.ShapeDtypeStruct((1, t), I32), jax.ShapeDtypeStruct((1, t), F32),
                   jax.ShapeDtypeStruct((1, t), F32)],
        compiler_params=_params("parallel"),
        name="outproj_router",
    )(tile_batch, o1[0], o2[0], o1[1], o2[1], x, mod4, mod4, mod4, g.reshape(1, d),
      w_out.reshape(2, half, d), wr_hi, wr_lo, br)


def _moe_kernel(bm, nsteps, ea_ref, eb_ref, nused_ref,
                src0_ref, src1_ref, src2_ref, dst_ref, h_ref, wa_ref, wb_ref,
                gua_ref, dna_ref, gub_ref, dnb_ref, y_ref, xbuf, ybuf, gsem, ssem):
    i = pl.program_id(0)
    n = nused_ref[0]
    gslot = i % 3
    sslot = i % 2
    ntok = y_ref.shape[0] - 2 * bm

    def gather_row(ids_ref, r, s):
        return pltpu.make_async_copy(h_ref.at[pl.ds(ids_ref[0, 0, r], 1), :],
                                     xbuf.at[s, pl.ds(r, 1), :], gsem.at[s])

    def gather_loop(ids_ref, s):
        def body(c, carry):
            for j in range(DMA_UNROLL):
                gather_row(ids_ref, c * DMA_UNROLL + j, s).start()
            return carry

        lax.fori_loop(0, bm // DMA_UNROLL, body, 0)

    def wait_rows(src, dst, sem):
        def body(c, carry):
            for _ in range(DMA_UNROLL):
                pltpu.make_async_copy(src.at[pl.ds(0, 1), :], dst.at[pl.ds(0, 1), :], sem).wait()
            return carry

        lax.fori_loop(0, bm // DMA_UNROLL, body, 0)

    def wait_gather(s):
        wait_rows(h_ref, xbuf.at[s], gsem.at[s])

    def wait_scatter(s):
        wait_rows(ybuf.at[s], y_ref, ssem.at[s])

    @pl.when(i == 0)
    def _():
        gather_loop(src0_ref, 0)
        gather_loop(src1_ref, 1)
        ybuf[0] = jnp.zeros((bm, ybuf.shape[2]), F32)
        spare = [pltpu.make_async_copy(ybuf.at[0], y_ref.at[pl.ds(ntok + k * bm, bm), :], ssem.at[k])
                 for k in range(2)]
        for c in spare:
            c.start()
        for c in spare:
            c.wait()

    @pl.when(i < n)
    def _():
        wait_gather(gslot)

        @pl.when(i >= 2)
        def _():
            wait_scatter(sslot)

    @pl.when(i < n)
    def _():
        x = xbuf[gslot].astype(BF16)

        def expert(gu_ref, dn_ref):
            gu = _dot(x, gu_ref[0])
            g, u = gu[:, :D_EXPERT], gu[:, D_EXPERT:]
            act = (g / (1.0 + jnp.exp(-g))) * u
            return _dot(act.astype(BF16), dn_ref[0])

        ybuf[sslot] = wa_ref[...] * expert(gua_ref, dna_ref) + wb_ref[...] * expert(gub_ref, dnb_ref)
        nxt = (i + 2) % 3
        for r in range(bm):
            gather_row(src2_ref, r, nxt).start()
        for r in range(bm):
            pltpu.make_async_copy(ybuf.at[sslot, pl.ds(r, 1), :],
                                  y_ref.at[pl.ds(dst_ref[0, 0, r], 1), :], ssem.at[sslot]).start(priority=1)

    @pl.when(i == nsteps - 1)
    def _():
        wait_gather(n % 3)
        wait_gather((n + 1) % 3)
        wait_scatter((n - 1) % 2)

        @pl.when(n >= 2)
        def _():
            wait_scatter(n % 2)


def moe_experts(h, row_src, row_dst, blk_a, blk_b, n_used, row_wa, row_wb, w_gu, w_dn):
    t, d = h.shape
    bm = MOE_BLOCK
    nrows = row_src.shape[0]
    nblk = nrows // bm
    col = pl.BlockSpec((bm, 1), lambda i, *_: (i, 0))
    ids = lambda f: pl.BlockSpec((1, 1, bm), lambda i, *_: (jnp.minimum(f(i), nblk - 1), 0, 0),
                                 memory_space=pltpu.SMEM)
    gu = lambda sel: pl.BlockSpec((1, d, 2 * D_EXPERT), lambda i, *s: (s[sel][i], 0, 0))
    dn = lambda sel: pl.BlockSpec((1, D_EXPERT, d), lambda i, *s: (s[sel][i], 0, 0))
    hbm = pl.BlockSpec(memory_space=pl.ANY)
    src3 = row_src.reshape(nblk, 1, bm)
    return pl.pallas_call(
        functools.partial(_moe_kernel, bm, nblk),
        grid_spec=pltpu.PrefetchScalarGridSpec(
            num_scalar_prefetch=3, grid=(nblk,),
            in_specs=[ids(lambda i: 0), ids(lambda i: 1), ids(lambda i: i + 2), ids(lambda i: i),
                      hbm, col, col, gu(0), dn(0), gu(1), dn(1)],
            out_specs=hbm,
            scratch_shapes=[pltpu.VMEM((3, bm, d), F32), pltpu.VMEM((2, bm, d), F32),
                            pltpu.SemaphoreType.DMA((3,)), pltpu.SemaphoreType.DMA((2,))]),
        out_shape=jax.ShapeDtypeStruct((t + 2 * bm, d), F32),
        compiler_params=_params("arbitrary"),
        name="moe_experts",
    )(blk_a, blk_b, n_used, src3, src3, src3, row_dst.reshape(nblk, 1, bm), h,
      row_wa.reshape(nrows, 1), row_wb.reshape(nrows, 1), w_gu, w_dn, w_gu, w_dn)


def _dispatch(cls, w_lo, w_hi, expert_base):
    t = cls.shape[0]
    bm = MOE_BLOCK
    nblk = -(-t // bm) + N_CLASSES
    nrows = nblk * bm
    iota = jnp.arange(t, dtype=I32)
    _, order, s_lo, s_hi = lax.sort((cls, iota, w_lo, w_hi), num_keys=1, is_stable=True)
    classes = jnp.arange(N_CLASSES, dtype=I32)
    counts = jnp.sum((cls[:, None] == classes[None, :]).astype(I32), axis=0)
    padded = (counts + bm - 1) // bm * bm
    pend = jnp.cumsum(padded)
    pstart = pend - padded
    start = jnp.cumsum(counts) - counts
    blk_start = jnp.arange(nblk, dtype=I32) * bm
    blk_cls = jnp.minimum(jnp.sum((pend[None, :] <= blk_start[:, None]).astype(I32), axis=-1),
                          N_CLASSES - 1)
    k0 = blk_start - pstart[blk_cls]
    k = k0[:, None] + jnp.arange(bm, dtype=I32)[None, :]
    valid = (k < counts[blk_cls][:, None]).reshape(nrows)
    src = jnp.clip(start[blk_cls][:, None] + k, 0, t - 1).reshape(nrows)
    tok = order[src]
    row_src = jnp.where(valid, tok, 0)
    spare = (t + (jnp.arange(nblk, dtype=I32) % 2)[:, None] * bm
             + jnp.arange(bm, dtype=I32)[None, :]).reshape(nrows)
    row_dst = jnp.where(valid, tok, spare)
    row_wa = jnp.where(valid, s_lo[src], 0.0)
    row_wb = jnp.where(valid, s_hi[src], 0.0)
    grp = blk_cls // N_PAIRS
    pair = blk_cls % N_PAIRS
    blk_a = expert_base + grp * EPG + jnp.asarray(PAIR_LO)[pair]
    blk_b = expert_base + grp * EPG + jnp.asarray(PAIR_HI)[pair]
    n_used = (pend[-1] // bm).reshape(1).astype(I32)
    return row_src, row_dst, row_wa, row_wb, blk_a.astype(I32), blk_b.astype(I32), n_used


def _final_kernel(tb_ref, x_ref, y_ref, g2_ref, g_ref, o_ref):
    x = x_ref[...] + g2_ref[0, 0] * y_ref[...]
    o_ref[...] = _rms(x, g_ref[...], EPS)


def final_norm(x, y, tile_batch, tile_off, nrows, mod4, g):
    d = x.shape[1]
    tm = ROW_TILE
    row = pl.BlockSpec((tm, d), lambda i, tb: (tile_off + i, 0))
    return pl.pallas_call(
        _final_kernel,
        grid_spec=pltpu.PrefetchScalarGridSpec(
            num_scalar_prefetch=1, grid=(nrows // tm,),
            in_specs=[row, row,
                      pl.BlockSpec((1, 1, 1, d), lambda i, tb: (tb[tile_off + i], 5, 0, 0)),
                      pl.BlockSpec((1, d), lambda i, tb: (0, 0))],
            out_specs=pl.BlockSpec((tm, d), lambda i, tb: (i, 0))),
        out_shape=jax.ShapeDtypeStruct((nrows, d), F32),
        compiler_params=_params("parallel"),
        name="final_norm",
    )(tile_batch, x, y, mod4, g.reshape(1, d))


def _tile_batch(tile, groups):
    out = []
    base = 0
    for nb, seq in groups:
        assert seq % tile == 0
        out.append(base + np.repeat(np.arange(nb), seq // tile))
        base += nb
    return jnp.asarray(np.concatenate(out), I32)


def kernel(x_prompt, x_sample, c_prompt, c_sample, w_ada, b_ada, norm_mix_g, norm_ffn_g, w_in_ab, w_out_ab, lambda_q1, lambda_k1, lambda_q2, lambda_k2, subln_g, na_bias, w_in_cd, conv_w, conv_b, w_out_cd, rel_bias, w_router, b_router, w_gate_up, w_down, final_norm_g):
    depth, d = norm_mix_g.shape
    groups = [(x_prompt.shape[0], x_prompt.shape[1]), (x_sample.shape[0], x_sample.shape[1])]
    rows = [nb * seq for nb, seq in groups]
    offs = [0, rows[0]]
    t = rows[0] + rows[1]
    nb_all = groups[0][0] + groups[1][0]
    assert nb_all <= NB_PAD

    x = (x_prompt.reshape(rows[0], d), x_sample.reshape(rows[1], d))
    c_all = jnp.concatenate([c_prompt, c_sample, jnp.zeros((NB_PAD - nb_all, d), F32)], axis=0)
    mod = ada_mod(c_all, w_ada, b_ada).reshape(depth, NB_PAD, 6, 1, d)
    tb_row = _tile_batch(ROW_TILE, groups)
    tb_route = _tile_batch(ROUTE_TILE, groups)
    tb_proj = _tile_batch(PROJ_TILE, groups)

    wr = w_router.astype(F32).reshape(d, N_GROUPS, EPG).transpose(0, 2, 1).reshape(d, N_EXPERTS)
    wr_hi, wr_lo = _split_bf16(jnp.pad(wr, ((0, 0), (0, LANES - N_EXPERTS))))
    br = b_router.astype(F32).reshape(N_GROUPS, EPG).T.reshape(N_EXPERTS, 1)

    cc, sc = _dft_channel_tables()
    w_gu = w_gate_up.astype(BF16).reshape((depth * N_EXPERTS,) + w_gate_up.shape[2:])
    w_dn = w_down.astype(BF16).reshape((depth * N_EXPERTS,) + w_down.shape[2:])
    y = None
    for l in range(depth):
        i = l // 2
        mod4 = mod[l]
        if l % 2 == 0:
            w_in, w_out = w_in_ab[i], w_out_ab[i]
        else:
            w_in, w_out = w_in_cd[i], w_out_cd[i]
        proj, x = norm_matmul(x, tb_proj, mod4, 1, 0, norm_mix_g[l], w_in.astype(BF16),
                              y=y, g2_mod=mod[l - 1] if l else None, g2_piece=5)
        if l % 2 == 0:
            lam_init = 0.8 - 0.6 * math.exp(-0.3 * l)
            lam = (jnp.exp(jnp.sum(lambda_q1[i].astype(F32) * lambda_k1[i].astype(F32)))
                   - jnp.exp(jnp.sum(lambda_q2[i].astype(F32) * lambda_k2[i].astype(F32)))
                   + lam_init).reshape(1)
            rb = rel_bias.astype(F32)
            far = jnp.stack([rb[T5_BUCKETS // 2 - 1], rb[T5_BUCKETS - 1]], axis=1)
            na_tab = _na_tables(na_bias[i])
            subg = subln_g[i].astype(F32).reshape(A_V, 1)
            o1, o2, bands = [], [], {}
            for (nb, seq), off in zip(groups, offs):
                tq = min(ATT_TQ, seq // 4)
                if tq not in bands:
                    bands[tq] = _t5_band_tables(rel_bias, tq)
                o1.append(diff_attention(proj, off, nb, seq, lam, far, bands[tq], subg, lam_init, tq))
                o2.append(neighbourhood_attention(proj, off, nb, seq, na_tab))
        else:
            o1, o2 = [], []
            for (nb, seq), off in zip(groups, offs):
                o1.append(fourier_mix(proj, off, nb, seq, cc, sc, _dft_position_table(seq)))
                o2.append(short_conv(proj, off, nb, seq, conv_w[i].astype(F32), conv_b[i].astype(F32)))
        x, h2, cls, w_lo, w_hi = outproj_router(o1, o2, rows[0], x, tb_route, mod4,
                                                w_out.astype(BF16), norm_ffn_g[l], wr_hi, wr_lo, br)
        row_src, row_dst, row_wa, row_wb, blk_a, blk_b, n_used = _dispatch(
            cls[0], w_lo[0], w_hi[0], l * N_EXPERTS)
        y = moe_experts(h2, row_src, row_dst, blk_a, blk_b, n_used, row_wa, row_wb, w_gu, w_dn)

    mod4 = mod[depth - 1]
    outs = []
    for (nb, seq), off, n in zip(groups, offs, rows):
        o = final_norm(x, y, tb_row, off // ROW_TILE, n, mod4, final_norm_g)
        outs.append(o.reshape(nb, seq, d))
    return tuple(outs)
```

```python
import functools
import math

import numpy as np
import jax
import jax.numpy as jnp
from jax import lax
from jax.experimental import pallas as pl
from jax.experimental.pallas import tpu as pltpu

F32 = jnp.float32
BF16 = jnp.bfloat16
I32 = jnp.int32

GRID_W = 64
A_HEADS = 8
A_QK = 64
A_V = 128
SUBLN_EPS = 1e-5
B_HEADS = 8
B_HD = 128
NA_WR = 8
NA_WC = 16
NA_QR = 8
NA_KR = 16
C_GROUPS = 4
C_GD = 256
C_WIDTH = 1024
D_WIDTH = 1024
T5_BUCKETS = 32
T5_MAX_DIST = 128
N_EXPERTS = 32
N_GROUPS = 8
EPG = 4
D_EXPERT = 512
EPS = 1e-6
N_PAIRS = 6
N_CLASSES = N_GROUPS * N_PAIRS
PAIR_LO = np.array([0, 0, 0, 1, 1, 2], np.int32)
PAIR_HI = np.array([1, 2, 3, 2, 3, 3], np.int32)

LANES = 128
VMEM_LIMIT = 56 * 1024 * 1024

NB_PAD = 16
ROW_TILE = 512
PROJ_TILE = 256
ROUTE_TILE = 256
MOE_BLOCK = 256
ATT_TQ = 256
ATT_TILES = 4
DMA_UNROLL = 8
SUM_ROWS = 16


def _dot(a, b):
    return jnp.dot(a, b, preferred_element_type=F32)


def _dot_nt(a, b):
    return lax.dot_general(a, b, (((1,), (1,)), ((), ())), preferred_element_type=F32)


def _split_bf16(x):
    hi = x.astype(BF16)
    lo = (x - hi.astype(F32)).astype(BF16)
    return hi, lo


def _params(*sem):
    return pltpu.CompilerParams(dimension_semantics=sem, vmem_limit_bytes=VMEM_LIMIT)


def _ada_kernel(c_ref, w_ref, b_ref, o_ref):
    c = c_ref[...]
    cs = c / (1.0 + jnp.exp(-c))
    cs_hi, cs_lo = _split_bf16(cs)
    w_hi, w_lo = _split_bf16(w_ref[0])
    acc = _dot(cs_hi, w_hi) + _dot(cs_lo, w_hi) + _dot(cs_hi, w_lo)
    o_ref[0] = acc + b_ref[0]


def ada_mod(c_all, w_ada, b_ada):
    depth, d, n = w_ada.shape
    tn = 1024
    return pl.pallas_call(
        _ada_kernel,
        grid=(depth, n // tn),
        in_specs=[
            pl.BlockSpec((NB_PAD, d), lambda l, j: (0, 0)),
            pl.BlockSpec((1, d, tn), lambda l, j: (l, 0, j)),
            pl.BlockSpec((1, 1, tn), lambda l, j: (l, 0, j)),
        ],
        out_specs=pl.BlockSpec((1, NB_PAD, tn), lambda l, j: (l, 0, j)),
        out_shape=jax.ShapeDtypeStruct((depth, NB_PAD, n), F32),
        compiler_params=_params("parallel", "parallel"),
        name="ada_mod",
    )(c_all, w_ada, b_ada.reshape(depth, 1, n))


def _rms(x, g, eps):
    return x * lax.rsqrt(jnp.mean(x * x, axis=-1, keepdims=True) + eps) * g


def _norm_matmul_kernel(n_first, tn, tb_ref, *refs):
    if n_first is None:
        x_ref, y_ref, g2_ref, sc_ref, sh_ref, g_ref, w_ref, o_ref, x2_ref, h_scr = refs
        x = x_ref[...] + g2_ref[0, 0] * y_ref[...]
    else:
        xa_ref, xb_ref, sc_ref, sh_ref, g_ref, w_ref, o_ref, x2_ref, h_scr = refs
        x = jnp.where(pl.program_id(0) < n_first, xa_ref[...], xb_ref[...])
    x2_ref[...] = x
    h = _rms(x, g_ref[...], EPS) * (1.0 + sc_ref[0, 0]) + sh_ref[0, 0]
    h_scr[...] = h.astype(BF16)
    for j in range(w_ref.shape[1] // tn):
        cols = slice(j * tn, (j + 1) * tn)
        o_ref[:, cols] = _dot(h_scr[...], w_ref[:, cols]).astype(o_ref.dtype)


def norm_matmul(x, tile_batch, mod4, sc_piece, sh_piece, g, w, y=None, g2_mod=None, g2_piece=None):
    n = w.shape[1]
    tm, tn = PROJ_TILE, 1024

    def piece(p):
        return pl.BlockSpec((1, 1, 1, d), lambda i, tb: (tb[i], p, 0, 0))

    if y is None:
        xa, xb = x
        d = xa.shape[1]
        t = xa.shape[0] + xb.shape[0]
        assert xa.shape[0] % tm == 0
        n_first = xa.shape[0] // tm
        in_specs = [pl.BlockSpec((tm, d), lambda i, tb: (jnp.minimum(i, n_first - 1), 0)),
                    pl.BlockSpec((tm, d), lambda i, tb: (jnp.maximum(i - n_first, 0), 0))]
        args = [xa, xb]
    else:
        t, d = x.shape
        n_first = None
        in_specs = [pl.BlockSpec((tm, d), lambda i, tb: (i, 0)),
                    pl.BlockSpec((tm, d), lambda i, tb: (i, 0)), piece(g2_piece)]
        args = [x, y, g2_mod]
    in_specs += [piece(sc_piece), piece(sh_piece),
                 pl.BlockSpec((1, d), lambda i, tb: (0, 0)),
                 pl.BlockSpec((d, n), lambda i, tb: (0, 0), pipeline_mode=pl.Buffered(1))]
    args += [mod4, mod4, g.reshape(1, d), w]
    return pl.pallas_call(
        functools.partial(_norm_matmul_kernel, n_first, tn),
        grid_spec=pltpu.PrefetchScalarGridSpec(
            num_scalar_prefetch=1, grid=(t // tm,),
            in_specs=in_specs,
            out_specs=[pl.BlockSpec((tm, n), lambda i, tb: (i, 0)),
                       pl.BlockSpec((tm, d), lambda i, tb: (i, 0))],
            scratch_shapes=[pltpu.VMEM((tm, d), BF16)]),
        out_shape=[jax.ShapeDtypeStruct((t, n), BF16), jax.ShapeDtypeStruct((t, d), F32)],
        compiler_params=_params("parallel"),
        name="norm_matmul",
    )(tile_batch, *args)


def _t5_bucket_np(rel):
    half = T5_BUCKETS // 2
    max_exact = half // 2
    sign = (rel > 0).astype(np.int64) * half
    n = np.abs(rel)
    nf = np.maximum(n, 1).astype(np.float64)
    large = max_exact + (np.log(nf / max_exact) / math.log(T5_MAX_DIST / max_exact)
                         * (half - max_exact)).astype(np.int64)
    large = np.minimum(large, half - 1)
    return (sign + np.where(n < max_exact, n, large)).astype(np.int32)


def _t5_band_tables(rel_bias, tq):
    d = np.arange(-1, 2)[:, None, None]
    kk = np.arange(tq)[None, :, None]
    qq = np.arange(tq)[None, None, :]
    bucket = jnp.asarray(_t5_bucket_np(d * tq + kk - qq)[None].astype(np.int8))
    rb = rel_bias.astype(F32).T[:, :, None, None, None]
    tab = jnp.zeros((A_HEADS, 3, tq, tq), F32)
    for b in range(T5_BUCKETS):
        tab = jnp.where(bucket == b, rb[:, b], tab)
    return tab


def _diffattn_kernel(seq, tq, lam_init, lam_ref, far_ref, q1_ref, q2_ref, k1_ref, k2_ref, v_ref,
                     band_ref, subg_ref, o_ref, vt_scr, p1_scr, p2_scr, acc_scr, *s_scr):
    s1_scr, s2_scr = s_scr[0::2], s_scr[1::2]
    h = pl.program_id(1)
    step = pl.program_id(2)
    nch = seq // tq
    grp = tq // 8
    ngrp, _, gkeys = vt_scr.shape
    gch = gkeys // tq

    @pl.when(step == 0)
    def _():
        for g in range(ngrp):
            vt_scr[g, :A_V, :] = v_ref[g * gkeys:(g + 1) * gkeys, :].astype(F32).T.astype(BF16)
            vt_scr[g, A_V:, :] = jnp.ones((SUM_ROWS, gkeys), BF16)

    lane = lax.broadcasted_iota(I32, (1, LANES), 1)
    head_mask = jnp.where((lane // A_QK) == (h % 2), A_QK ** -0.5, 0.0).astype(BF16)
    far_l, far_r = far_ref[h, 0], far_ref[h, 1]
    neg = jnp.full((8, tq), -jnp.inf, F32)

    def fold(x):
        return x.reshape(grp, 8, tq)

    def scores(t, c, m):
        qrows = slice(t * tq, (t + 1) * tq)
        krows = pl.ds(pl.multiple_of(c * tq, tq), tq)
        d = c - (step * ATT_TILES + t)
        near = band_ref[0, jnp.clip(d + 1, 0, 2)]
        bias = jnp.where(jnp.abs(d) <= 1, near, jnp.where(d < 0, far_l, far_r))
        s1 = _dot_nt(k1_ref[krows, :], q1_ref[qrows, :] * head_mask) + bias
        s2 = _dot_nt(k2_ref[krows, :], q2_ref[qrows, :] * head_mask) + bias
        s1_scr[t][krows, :] = s1
        s2_scr[t][krows, :] = s2
        return (jnp.maximum(m[0], jnp.max(fold(s1), axis=0)),
                jnp.maximum(m[1], jnp.max(fold(s2), axis=0)))

    def col_max(m):
        return tuple(jnp.max(x, axis=0, keepdims=True) for x in m)

    def values(t, g, m):
        if isinstance(g, int):
            rows = slice(g * gkeys, (g + 1) * gkeys)
        else:
            rows = pl.ds(pl.multiple_of(g * gkeys, gkeys), gkeys)
        p1_scr[rows, :] = jnp.exp((s1_scr[t][rows, :] - m[0]).astype(BF16))
        p2_scr[rows, :] = jnp.exp((s2_scr[t][rows, :] - m[1]).astype(BF16))
        acc_scr[0] += _dot(vt_scr[g], p1_scr[rows, :])
        acc_scr[1] += _dot(vt_scr[g], p2_scr[rows, :])

    def finish(t):
        n1 = 1.0 / acc_scr[0, A_V:A_V + 1, :]
        n2 = lam_ref[0] / acc_scr[1, A_V:A_V + 1, :]
        ot = acc_scr[0, :A_V, :] * n1 - acc_scr[1, :A_V, :] * n2
        ms = jnp.mean(ot * ot, axis=0, keepdims=True)
        ot = ot * lax.rsqrt(ms + SUBLN_EPS) * (subg_ref[...] * (1.0 - lam_init))
        o_ref[t * tq:(t + 1) * tq, :] = ot.T.astype(o_ref.dtype)

    m = col_max(lax.fori_loop(0, nch, lambda c, mm: scores(0, c, mm), (neg, neg), unroll=min(nch, 8)))
    for t in range(ATT_TILES):
        acc_scr[...] = jnp.zeros_like(acc_scr)
        if t + 1 < ATT_TILES:
            def mixed(g, mm, t=t, m=m):
                values(t, g, m)
                for j in range(gch):
                    mm = scores(t + 1, g * gch + j, mm)
                return mm

            m_next = col_max(lax.fori_loop(0, ngrp, mixed, (neg, neg), unroll=min(ngrp, 2)))
        else:
            for g in range(ngrp):
                values(t, g, m)
        finish(t)
        if t + 1 < ATT_TILES:
            m = m_next


def diff_attention(proj, row_off, nbatch, seq, lam, far, band, subg_col, lam_init, tq):
    nq = seq // tq
    gch = 4
    tiles = ATT_TILES
    assert tq >= LANES and nq >= 3 and nq % gch == 0 and nq % tiles == 0 and row_off % seq == 0
    ns = nq // tiles
    boff = row_off // seq
    qoff = row_off // (tiles * tq)
    na = A_HEADS * A_QK // LANES

    def qspec(sec):
        return pl.BlockSpec((tiles * tq, LANES), lambda b, h, i: (qoff + b * ns + i, sec * na + h // 2))

    def kspec(sec):
        return pl.BlockSpec((seq, LANES), lambda b, h, i: (boff + b, sec * na + h // 2))

    smem = pl.BlockSpec(memory_space=pltpu.SMEM)
    return pl.pallas_call(
        functools.partial(_diffattn_kernel, seq, tq, lam_init),
        grid=(nbatch, A_HEADS, ns),
        in_specs=[smem, smem, qspec(0), qspec(1), kspec(2), kspec(3),
                  pl.BlockSpec((seq, A_V), lambda b, h, i: (boff + b, 4 * na + h)),
                  pl.BlockSpec((1, 3, tq, tq), lambda b, h, i: (h, 0, 0, 0)),
                  pl.BlockSpec((A_V, 1), lambda b, h, i: (0, 0))],
        out_specs=pl.BlockSpec((tiles * tq, A_V), lambda b, h, i: (b * ns + i, h)),
        out_shape=jax.ShapeDtypeStruct((nbatch * seq, A_HEADS * A_V), BF16),
        scratch_shapes=[pltpu.VMEM((nq // gch, A_V + SUM_ROWS, gch * tq), BF16),
                        pltpu.VMEM((seq, tq), BF16), pltpu.VMEM((seq, tq), BF16),
                        pltpu.VMEM((2, A_V + SUM_ROWS, tq), F32)]
                       + [pltpu.VMEM((seq, tq), F32)] * (2 * tiles),
        compiler_params=_params("parallel", "parallel", "arbitrary"),
        name="diff_attention",
    )(lam, far, proj, proj, proj, proj, proj, band, subg_col)


def _na_tables(na_tab):
    half = NA_WR // 2
    qr = np.arange(NA_QR)[:, None]
    kr = np.arange(NA_KR)[None, :]
    dr = np.full((3, NA_QR, NA_KR), -1, np.int64)
    for var, (rs, off) in enumerate([(np.maximum(qr - half, 0), NA_WR - 1),
                                     (qr, half - 1),
                                     (half + np.minimum(qr, half), -1)]):
        dr[var] = np.where((kr >= rs) & (kr < rs + NA_WR), kr - qr + off, -1)
    c = np.arange(GRID_W)[:, None]
    kc = np.arange(GRID_W)[None, :]
    cs = np.clip(c - NA_WC // 2, 0, GRID_W - NA_WC)
    col_ok = (kc >= cs) & (kc < cs + NA_WC)
    dc = jnp.asarray(np.where(col_ok, kc - c + NA_WC - 1, -1).astype(np.int8))[None, None]
    na = na_tab.astype(F32)
    cols = jnp.zeros((B_HEADS, 2 * NA_WR - 1, GRID_W, GRID_W), F32)
    for s in range(2 * NA_WC - 1):
        cols = jnp.where(dc == s, na[:, :, s][:, :, None, None], cols)
    onehot = (dr[..., None] == np.arange(2 * NA_WR - 1)).astype(np.float32)
    tab = jnp.einsum('vqkr,hrcl->hvqckl', jnp.asarray(onehot), cols, precision=lax.Precision.HIGHEST)
    ok = jnp.asarray(dr >= 0)[None, :, :, None, :, None] & jnp.asarray(col_ok)[None, None, None, :, None, :]
    tab = jnp.where(ok, tab, -1e30)
    return tab.reshape(B_HEADS, 3, NA_QR * GRID_W, NA_KR * GRID_W)


def _na_kernel(rows, q_ref, k_ref, v_ref, tab_ref, o_ref):
    scale = B_HD ** -0.5
    nq = NA_QR * GRID_W
    nk = NA_KR * GRID_W

    def body(j, carry):
        r0 = j * NA_QR
        kr0 = jnp.clip(r0 - NA_WR // 2, 0, rows - NA_KR)
        var = lax.shift_right_logical(r0 - kr0, 2)
        qs = pl.multiple_of(r0 * GRID_W, nq)
        ks = pl.multiple_of(kr0 * GRID_W, (NA_WR // 2) * GRID_W)
        s = _dot_nt(q_ref[pl.ds(qs, nq), :], k_ref[pl.ds(ks, nk), :]) * scale + tab_ref[0, var]
        p = jnp.exp(s - jnp.max(s, axis=-1, keepdims=True))
        inv = 1.0 / jnp.sum(p, axis=-1, keepdims=True)
        o_ref[pl.ds(qs, nq), :] = (_dot(p.astype(BF16), v_ref[pl.ds(ks, nk), :]) * inv).astype(o_ref.dtype)
        return carry

    lax.fori_loop(0, rows // NA_QR, body, 0, unroll=2)


def neighbourhood_attention(proj, row_off, nbatch, seq, tab):
    rows = seq // GRID_W
    assert rows >= NA_KR and rows % NA_QR == 0 and NA_WR == 8 and row_off % seq == 0
    boff = row_off // seq
    base = (4 * A_HEADS * A_QK + A_HEADS * A_V) // LANES
    nb = B_HEADS * B_HD // LANES

    def spec(sec):
        return pl.BlockSpec((seq, B_HD), lambda b, h: (boff + b, base + sec * nb + h))

    return pl.pallas_call(
        functools.partial(_na_kernel, rows),
        grid=(nbatch, B_HEADS),
        in_specs=[spec(0), spec(1), spec(2),
                  pl.BlockSpec((1, 3, NA_QR * GRID_W, NA_KR * GRID_W), lambda b, h: (h, 0, 0, 0))],
        out_specs=pl.BlockSpec((seq, B_HD), lambda b, h: (b, h)),
        out_shape=jax.ShapeDtypeStruct((nbatch * seq, B_HEADS * B_HD), BF16),
        compiler_params=_params("parallel", "parallel"),
        name="neighbourhood_attention",
    )(proj, proj, proj, tab)


def _dft_channel_tables():
    j = np.arange(C_GD)
    ang = 2.0 * np.pi * ((j[:, None] * j[None, :]) % C_GD) / C_GD
    s = C_GD ** -0.5
    return jnp.asarray(np.cos(ang) * s, BF16), jnp.asarray(np.sin(ang) * s, BF16)


def _dft_position_table(seq):
    r = 64
    k = jnp.arange(seq, dtype=I32)
    jh = jnp.arange(seq // r, dtype=I32)
    jl = jnp.arange(r, dtype=I32)
    step = 2.0 * np.pi / seq
    a = ((jh[:, None] * r * k[None, :]) % seq).astype(F32) * step
    b = ((jl[:, None] * k[None, :]) % seq).astype(F32) * step
    ca, sa = jnp.cos(a)[:, None, :], jnp.sin(a)[:, None, :]
    cb, sb = jnp.cos(b)[None, :, :], jnp.sin(b)[None, :, :]
    scale = seq ** -0.5
    c = ((ca * cb - sa * sb) * scale).reshape(seq, seq)
    s = ((sa * cb + ca * sb) * -scale).reshape(seq, seq)
    return jnp.concatenate([c, s], axis=1).astype(BF16)


def _fourier_channel_kernel(u_ref, cc_ref, sc_ref, o_ref):
    for g in range(C_GROUPS):
        sl = slice(g * C_GD, (g + 1) * C_GD)
        ug = u_ref[:, sl]
        o_ref[0, 0, :, sl] = _dot(ug, cc_ref[...]).astype(o_ref.dtype)
        o_ref[0, 1, :, sl] = _dot(ug, sc_ref[...]).astype(o_ref.dtype)


def _matmul_kernel(a_ref, b_ref, o_ref):
    o_ref[...] = _dot(a_ref[...], b_ref[0]).astype(o_ref.dtype)


def fourier_mix(proj, row_off, nbatch, seq, cc, sc, pos_tab):
    tm = 512
    assert row_off % tm == 0 and seq % tm == 0
    ns = seq // tm
    roff = row_off // tm
    v = pl.pallas_call(
        _fourier_channel_kernel,
        grid=(nbatch, ns),
        in_specs=[pl.BlockSpec((tm, C_WIDTH), lambda b, i: (roff + b * ns + i, 0)),
                  pl.BlockSpec((C_GD, C_GD), lambda b, i: (0, 0)),
                  pl.BlockSpec((C_GD, C_GD), lambda b, i: (0, 0))],
        out_specs=pl.BlockSpec((1, 2, tm, C_WIDTH), lambda b, i: (b, 0, i, 0)),
        out_shape=jax.ShapeDtypeStruct((nbatch, 2, seq, C_WIDTH), BF16),
        compiler_params=_params("parallel", "parallel"),
        name="fourier_channels",
    )(proj, cc, sc)
    v = v.reshape(nbatch, 2 * seq, C_WIDTH)
    tn = 512
    return pl.pallas_call(
        _matmul_kernel,
        grid=(ns, nbatch, C_WIDTH // tn),
        in_specs=[pl.BlockSpec((tm, 2 * seq), lambda i, b, j: (i, 0)),
                  pl.BlockSpec((1, 2 * seq, tn), lambda i, b, j: (b, 0, j))],
        out_specs=pl.BlockSpec((tm, tn), lambda i, b, j: (b * ns + i, j)),
        out_shape=jax.ShapeDtypeStruct((nbatch * seq, C_WIDTH), BF16),
        compiler_params=_params("parallel", "parallel", "parallel"),
        name="fourier_positions",
    )(pos_tab, v)


def _conv_kernel(bg_ref, cg_ref, hv_ref, cw_ref, cb_ref, o_ref):
    u = cg_ref[...].astype(F32) * hv_ref[...].astype(F32)
    seq = u.shape[0]
    row = lax.broadcasted_iota(I32, u.shape, 0)
    prev = jnp.where(row == 0, 0.0, pltpu.roll(u, 1, 0))
    nxt = jnp.where(row == seq - 1, 0.0, pltpu.roll(u, seq - 1, 0))
    y = prev * cw_ref[0:1, :] + u * cw_ref[1:2, :] + nxt * cw_ref[2:3, :] + cb_ref[...]
    o_ref[...] = (bg_ref[...].astype(F32) * y).astype(o_ref.dtype)


def short_conv(proj, row_off, nbatch, seq, cw, cb):
    tc = 256
    assert row_off % seq == 0
    boff = row_off // seq
    nct = D_WIDTH // tc
    base = C_WIDTH // tc

    def spec(sec):
        return pl.BlockSpec((seq, tc), lambda b, j: (boff + b, base + sec * nct + j))

    return pl.pallas_call(
        _conv_kernel,
        grid=(nbatch, nct),
        in_specs=[spec(0), spec(1), spec(2),
                  pl.BlockSpec((3, tc), lambda b, j: (0, j)),
                  pl.BlockSpec((1, tc), lambda b, j: (0, j))],
        out_specs=pl.BlockSpec((seq, tc), lambda b, j: (b, j)),
        out_shape=jax.ShapeDtypeStruct((nbatch * seq, D_WIDTH), BF16),
        compiler_params=_params("parallel", "parallel"),
        name="short_conv",
    )(proj, proj, proj, cw, cb.reshape(1, D_WIDTH))


def _route(logits, bias):
    s = 1.0 / (1.0 + jnp.exp(-logits))
    sel = s + bias
    sv = [s[j * N_GROUPS:(j + 1) * N_GROUPS] for j in range(EPG)]
    cv = [sel[j * N_GROUPS:(j + 1) * N_GROUPS] for j in range(EPG)]
    hi1, lo1 = jnp.maximum(cv[0], cv[1]), jnp.minimum(cv[0], cv[1])
    hi2, lo2 = jnp.maximum(cv[2], cv[3]), jnp.minimum(cv[2], cv[3])
    gscore = jnp.maximum(hi1, hi2) + jnp.maximum(jnp.minimum(hi1, hi2), jnp.maximum(lo1, lo2))
    gidx = lax.broadcasted_iota(I32, gscore.shape, 0).astype(F32)
    gmax = jnp.max(gscore, axis=0, keepdims=True)
    best = jnp.min(jnp.where(gscore == gmax, gidx, float(N_GROUPS)), axis=0, keepdims=True)
    onehot = gidx == best

    def pick(z):
        return jnp.sum(jnp.where(onehot, z, 0.0), axis=0, keepdims=True)

    c = [pick(z) for z in cv]
    w = [pick(z) for z in sv]
    i1, v1 = jnp.zeros_like(best), c[0]
    for j in range(1, EPG):
        better = c[j] > v1
        i1 = jnp.where(better, float(j), i1)
        v1 = jnp.where(better, c[j], v1)
    i2, v2 = jnp.full_like(best, -1.0), jnp.full_like(best, -jnp.inf)
    for j in range(EPG):
        better = (i1 != float(j)) & ((c[j] > v2) | (i2 < 0.0))
        i2 = jnp.where(better, float(j), i2)
        v2 = jnp.where(better, c[j], v2)
    lo = jnp.minimum(i1, i2)
    hi = jnp.maximum(i1, i2)

    def gate(idx):
        g = w[0]
        for j in range(1, EPG):
            g = jnp.where(idx == float(j), w[j], g)
        return g

    w_lo, w_hi = gate(lo), gate(hi)
    total = w_lo + w_hi
    pair = jnp.where(lo == 0.0, 0.0, jnp.where(lo == 1.0, 3.0, 5.0)) + (hi - lo - 1.0)
    cls = (best * float(N_PAIRS) + pair).astype(I32)
    return cls, w_lo / total, w_hi / total


def _outproj_router_kernel(n1, tb_ref, a1_ref, b1_ref, a2_ref, b2_ref, x_ref, g1_ref, sc_ref, sh_ref,
                           g_ref, w_ref, wrh_ref, wrl_ref, br_ref,
                           xo_ref, h_ref, cls_ref, wlo_ref, whi_ref):
    first = pl.program_id(0) < n1
    o1 = jnp.where(first, a1_ref[...], a2_ref[...])
    o2 = jnp.where(first, b1_ref[...], b2_ref[...])
    m = _dot(o1, w_ref[0]) + _dot(o2, w_ref[1])
    x = x_ref[...] + g1_ref[0, 0] * m
    xo_ref[...] = x
    h = _rms(x, g_ref[...], EPS) * (1.0 + sc_ref[0, 0]) + sh_ref[0, 0]
    h_ref[...] = h
    h_hi, h_lo = _split_bf16(h)
    logits = _dot(h_hi, wrh_ref[...]) + _dot(h_lo, wrh_ref[...]) + _dot(h_hi, wrl_ref[...])
    cls, w_lo, w_hi = _route(logits.T[:N_EXPERTS], br_ref[...])
    cls_ref[...] = cls
    wlo_ref[...] = w_lo
    whi_ref[...] = w_hi


def outproj_router(o1, o2, n_first, x, tile_batch, mod4, w_out, g, wr_hi, wr_lo, br):
    t, d = x.shape
    tm = ROUTE_TILE
    assert n_first % tm == 0
    n1 = n_first // tm
    half = w_out.shape[0] // 2

    def first(i, tb):
        return (jnp.minimum(i, n1 - 1), 0)

    def second(i, tb):
        return (jnp.maximum(i - n1, 0), 0)

    def piece(p):
        return pl.BlockSpec((1, 1, 1, d), lambda i, tb: (tb[i], p, 0, 0))

    row = pl.BlockSpec((tm, d), lambda i, tb: (i, 0))
    vec = pl.BlockSpec((1, tm), lambda i, tb: (0, i))
    full = lambda shape: pl.BlockSpec(shape, lambda i, tb: (0,) * len(shape))
    return pl.pallas_call(
        functools.partial(_outproj_router_kernel, n1),
        grid_spec=pltpu.PrefetchScalarGridSpec(
            num_scalar_prefetch=1, grid=(t // tm,),
            in_specs=[pl.BlockSpec((tm, half), first), pl.BlockSpec((tm, half), first),
                      pl.BlockSpec((tm, half), second), pl.BlockSpec((tm, half), second),
                      row, piece(2), piece(4), piece(3), full((1, d)),
                      full((2, half, d)), full((d, LANES)), full((d, LANES)),
                      full((N_EXPERTS, 1))],
            out_specs=[row, row, vec, vec, vec]),
        out_shape=[jax.ShapeDtypeStruct((t, d), F32), jax.ShapeDtypeStruct((t, d), F32),
                   jax.ShapeDtypeStruct((1, t), I32), jax.ShapeDtypeStruct((1, t), F32),
                   jax.ShapeDtypeStruct((1, t), F32)],
        compiler_params=_params("parallel"),
        name="outproj_router",
    )(tile_batch, o1[0], o2[0], o1[1], o2[1], x, mod4, mod4, mod4, g.reshape(1, d),
      w_out.reshape(2, half, d), wr_hi, wr_lo, br)


def _moe_kernel(bm, nsteps, ea_ref, eb_ref, nused_ref,
                src0_ref, src1_ref, src2_ref, dst_ref, h_ref, wa_ref, wb_ref,
                gua_ref, dna_ref, gub_ref, dnb_ref, y_ref, xbuf, ybuf, gsem, ssem):
    i = pl.program_id(0)
    n = nused_ref[0]
    gslot = i % 3
    sslot = i % 2
    ntok = y_ref.shape[0] - 2 * bm

    def gather_row(ids_ref, r, s):
        return pltpu.make_async_copy(h_ref.at[pl.ds(ids_ref[0, 0, r], 1), :],
                                     xbuf.at[s, pl.ds(r, 1), :], gsem.at[s])

    def gather_loop(ids_ref, s):
        def body(c, carry):
            for j in range(DMA_UNROLL):
                gather_row(ids_ref, c * DMA_UNROLL + j, s).start()
            return carry

        lax.fori_loop(0, bm // DMA_UNROLL, body, 0)

    def wait_rows(src, dst, sem):
        def body(c, carry):
            for _ in range(DMA_UNROLL):
                pltpu.make_async_copy(src.at[pl.ds(0, 1), :], dst.at[pl.ds(0, 1), :], sem).wait()
            return carry

        lax.fori_loop(0, bm // DMA_UNROLL, body, 0)

    def wait_gather(s):
        wait_rows(h_ref, xbuf.at[s], gsem.at[s])

    def wait_scatter(s):
        wait_rows(ybuf.at[s], y_ref, ssem.at[s])

    @pl.when(i == 0)
    def _():
        gather_loop(src0_ref, 0)
        gather_loop(src1_ref, 1)
        ybuf[0] = jnp.zeros((bm, ybuf.shape[2]), F32)
        spare = [pltpu.make_async_copy(ybuf.at[0], y_ref.at[pl.ds(ntok + k * bm, bm), :], ssem.at[k])
                 for k in range(2)]
        for c in spare:
            c.start()
        for c in spare:
            c.wait()

    @pl.when(i < n)
    def _():
        wait_gather(gslot)

        @pl.when(i >= 2)
        def _():
            wait_scatter(sslot)

    @pl.when(i < n)
    def _():
        x = xbuf[gslot].astype(BF16)

        def expert(gu_ref, dn_ref):
            gu = _dot(x, gu_ref[0])
            g, u = gu[:, :D_EXPERT], gu[:, D_EXPERT:]
            act = (g / (1.0 + jnp.exp(-g))) * u
            return _dot(act.astype(BF16), dn_ref[0])

        ybuf[sslot] = wa_ref[...] * expert(gua_ref, dna_ref) + wb_ref[...] * expert(gub_ref, dnb_ref)
        nxt = (i + 2) % 3
        for r in range(bm):
            gather_row(src2_ref, r, nxt).start()
        for r in range(bm):
            pltpu.make_async_copy(ybuf.at[sslot, pl.ds(r, 1), :],
                                  y_ref.at[pl.ds(dst_ref[0, 0, r], 1), :], ssem.at[sslot]).start(priority=1)

    @pl.when(i == nsteps - 1)
    def _():
        wait_gather(n % 3)
        wait_gather((n + 1) % 3)
        wait_scatter((n - 1) % 2)

        @pl.when(n >= 2)
        def _():
            wait_scatter(n % 2)


def moe_experts(h, row_src, row_dst, blk_a, blk_b, n_used, row_wa, row_wb, w_gu, w_dn):
    t, d = h.shape
    bm = MOE_BLOCK
    nrows = row_src.shape[0]
    nblk = nrows // bm
    col = pl.BlockSpec((bm, 1), lambda i, *_: (i, 0))
    ids = lambda f: pl.BlockSpec((1, 1, bm), lambda i, *_: (jnp.minimum(f(i), nblk - 1), 0, 0),
                                 memory_space=pltpu.SMEM)
    gu = lambda sel: pl.BlockSpec((1, d, 2 * D_EXPERT), lambda i, *s: (s[sel][i], 0, 0))
    dn = lambda sel: pl.BlockSpec((1, D_EXPERT, d), lambda i, *s: (s[sel][i], 0, 0))
    hbm = pl.BlockSpec(memory_space=pl.ANY)
    src3 = row_src.reshape(nblk, 1, bm)
    return pl.pallas_call(
        functools.partial(_moe_kernel, bm, nblk),
        grid_spec=pltpu.PrefetchScalarGridSpec(
            num_scalar_prefetch=3, grid=(nblk,),
            in_specs=[ids(lambda i: 0), ids(lambda i: 1), ids(lambda i: i + 2), ids(lambda i: i),
                      hbm, col, col, gu(0), dn(0), gu(1), dn(1)],
            out_specs=hbm,
            scratch_shapes=[pltpu.VMEM((3, bm, d), F32), pltpu.VMEM((2, bm, d), F32),
                            pltpu.SemaphoreType.DMA((3,)), pltpu.SemaphoreType.DMA((2,))]),
        out_shape=jax.ShapeDtypeStruct((t + 2 * bm, d), F32),
        compiler_params=_params("arbitrary"),
        name="moe_experts",
    )(blk_a, blk_b, n_used, src3, src3, src3, row_dst.reshape(nblk, 1, bm), h,
      row_wa.reshape(nrows, 1), row_wb.reshape(nrows, 1), w_gu, w_dn, w_gu, w_dn)


def _dispatch(cls, w_lo, w_hi, expert_base):
    t = cls.shape[0]
    bm = MOE_BLOCK
    nblk = -(-t // bm) + N_CLASSES
    nrows = nblk * bm
    iota = jnp.arange(t, dtype=I32)
    _, order, s_lo, s_hi = lax.sort((cls, iota, w_lo, w_hi), num_keys=1, is_stable=True)
    classes = jnp.arange(N_CLASSES, dtype=I32)
    counts = jnp.sum((cls[:, None] == classes[None, :]).astype(I32), axis=0)
    padded = (counts + bm - 1) // bm * bm
    pend = jnp.cumsum(padded)
    pstart = pend - padded
    start = jnp.cumsum(counts) - counts
    blk_start = jnp.arange(nblk, dtype=I32) * bm
    blk_cls = jnp.minimum(jnp.sum((pend[None, :] <= blk_start[:, None]).astype(I32), axis=-1),
                          N_CLASSES - 1)
    k0 = blk_start - pstart[blk_cls]
    k = k0[:, None] + jnp.arange(bm, dtype=I32)[None, :]
    valid = (k < counts[blk_cls][:, None]).reshape(nrows)
    src = jnp.clip(start[blk_cls][:, None] + k, 0, t - 1).reshape(nrows)
    tok = order[src]
    row_src = jnp.where(valid, tok, 0)
    spare = (t + (jnp.arange(nblk, dtype=I32) % 2)[:, None] * bm
             + jnp.arange(bm, dtype=I32)[None, :]).reshape(nrows)
    row_dst = jnp.where(valid, tok, spare)
    row_wa = jnp.where(valid, s_lo[src], 0.0)
    row_wb = jnp.where(valid, s_hi[src], 0.0)
    grp = blk_cls // N_PAIRS
    pair = blk_cls % N_PAIRS
    blk_a = expert_base + grp * EPG + jnp.asarray(PAIR_LO)[pair]
    blk_b = expert_base + grp * EPG + jnp.asarray(PAIR_HI)[pair]
    n_used = (pend[-1] // bm).reshape(1).astype(I32)
    return row_src, row_dst, row_wa, row_wb, blk_a.astype(I32), blk_b.astype(I32), n_used


def _final_kernel(tb_ref, x_ref, y_ref, g2_ref, g_ref, o_ref):
    x = x_ref[...] + g2_ref[0, 0] * y_ref[...]
    o_ref[...] = _rms(x, g_ref[...], EPS)


def final_norm(x, y, tile_batch, tile_off, nrows, mod4, g):
    d = x.shape[1]
    tm = ROW_TILE
    row = pl.BlockSpec((tm, d), lambda i, tb: (tile_off + i, 0))
    return pl.pallas_call(
        _final_kernel,
        grid_spec=pltpu.PrefetchScalarGridSpec(
            num_scalar_prefetch=1, grid=(nrows // tm,),
            in_specs=[row, row,
                      pl.BlockSpec((1, 1, 1, d), lambda i, tb: (tb[tile_off + i], 5, 0, 0)),
                      pl.BlockSpec((1, d), lambda i, tb: (0, 0))],
            out_specs=pl.BlockSpec((tm, d), lambda i, tb: (i, 0))),
        out_shape=jax.ShapeDtypeStruct((nrows, d), F32),
        compiler_params=_params("parallel"),
        name="final_norm",
    )(tile_batch, x, y, mod4, g.reshape(1, d))


def _tile_batch(tile, groups):
    out = []
    base = 0
    for nb, seq in groups:
        assert seq % tile == 0
        out.append(base + np.repeat(np.arange(nb), seq // tile))
        base += nb
    return jnp.asarray(np.concatenate(out), I32)


def kernel(x_prompt, x_sample, c_prompt, c_sample, w_ada, b_ada, norm_mix_g, norm_ffn_g, w_in_ab, w_out_ab, lambda_q1, lambda_k1, lambda_q2, lambda_k2, subln_g, na_bias, w_in_cd, conv_w, conv_b, w_out_cd, rel_bias, w_router, b_router, w_gate_up, w_down, final_norm_g):
    depth, d = norm_mix_g.shape
    groups = [(x_prompt.shape[0], x_prompt.shape[1]), (x_sample.shape[0], x_sample.shape[1])]
    rows = [nb * seq for nb, seq in groups]
    offs = [0, rows[0]]
    t = rows[0] + rows[1]
    nb_all = groups[0][0] + groups[1][0]
    assert nb_all <= NB_PAD

    x = (x_prompt.reshape(rows[0], d), x_sample.reshape(rows[1], d))
    c_all = jnp.concatenate([c_prompt, c_sample, jnp.zeros((NB_PAD - nb_all, d), F32)], axis=0)
    mod = ada_mod(c_all, w_ada, b_ada).reshape(depth, NB_PAD, 6, 1, d)
    tb_row = _tile_batch(ROW_TILE, groups)
    tb_route = _tile_batch(ROUTE_TILE, groups)
    tb_proj = _tile_batch(PROJ_TILE, groups)

    wr = w_router.astype(F32).reshape(d, N_GROUPS, EPG).transpose(0, 2, 1).reshape(d, N_EXPERTS)
    wr_hi, wr_lo = _split_bf16(jnp.pad(wr, ((0, 0), (0, LANES - N_EXPERTS))))
    br = b_router.astype(F32).reshape(N_GROUPS, EPG).T.reshape(N_EXPERTS, 1)

    cc, sc = _dft_channel_tables()
    w_gu = w_gate_up.astype(BF16).reshape((depth * N_EXPERTS,) + w_gate_up.shape[2:])
    w_dn = w_down.astype(BF16).reshape((depth * N_EXPERTS,) + w_down.shape[2:])
    y = None
    for l in range(depth):
        i = l // 2
        mod4 = mod[l]
        if l % 2 == 0:
            w_in, w_out = w_in_ab[i], w_out_ab[i]
        else:
            w_in, w_out = w_in_cd[i], w_out_cd[i]
        proj, x = norm_matmul(x, tb_proj, mod4, 1, 0, norm_mix_g[l], w_in.astype(BF16),
                              y=y, g2_mod=mod[l - 1] if l else None, g2_piece=5)
        if l % 2 == 0:
            lam_init = 0.8 - 0.6 * math.exp(-0.3 * l)
            lam = (jnp.exp(jnp.sum(lambda_q1[i].astype(F32) * lambda_k1[i].astype(F32)))
                   - jnp.exp(jnp.sum(lambda_q2[i].astype(F32) * lambda_k2[i].astype(F32)))
                   + lam_init).reshape(1)
            rb = rel_bias.astype(F32)
            far = jnp.stack([rb[T5_BUCKETS // 2 - 1], rb[T5_BUCKETS - 1]], axis=1)
            na_tab = _na_tables(na_bias[i])
            subg = subln_g[i].astype(F32).reshape(A_V, 1)
            o1, o2, bands = [], [], {}
            for (nb, seq), off in zip(groups, offs):
                tq = min(ATT_TQ, seq // 4)
                if tq not in bands:
                    bands[tq] = _t5_band_tables(rel_bias, tq)
                o1.append(diff_attention(proj, off, nb, seq, lam, far, bands[tq], subg, lam_init, tq))
                o2.append(neighbourhood_attention(proj, off, nb, seq, na_tab))
        else:
            o1, o2 = [], []
            for (nb, seq), off in zip(groups, offs):
                o1.append(fourier_mix(proj, off, nb, seq, cc, sc, _dft_position_table(seq)))
                o2.append(short_conv(proj, off, nb, seq, conv_w[i].astype(F32), conv_b[i].astype(F32)))
        x, h2, cls, w_lo, w_hi = outproj_router(o1, o2, rows[0], x, tb_route, mod4,
                                                w_out.astype(BF16), norm_ffn_g[l], wr_hi, wr_lo, br)
        row_src, row_dst, row_wa, row_wb, blk_a, blk_b, n_used = _dispatch(
            cls[0], w_lo[0], w_hi[0], l * N_EXPERTS)
        y = moe_experts(h2, row_src, row_dst, blk_a, blk_b, n_used, row_wa, row_wb, w_gu, w_dn)

    mod4 = mod[depth - 1]
    outs = []
    for (nb, seq), off, n in zip(groups, offs, rows):
        o = final_norm(x, y, tb_row, off // ROW_TILE, n, mod4, final_norm_g)
        outs.append(o.reshape(nb, seq, d))
    return tuple(outs)
```
